```python
import jax, jax.numpy as jnp
from jax import lax
import numpy as np

D_MODEL = 1024
BATCH = 16
SEQ = 4096
DEPTH = 1

CTX_LEN = 256
GRID_W = 64
HEAD_DIM = 64
NA_HEADS = 8
RW_HEADS = 8
NA_WIDTH = NA_HEADS * HEAD_DIM
RW_WIDTH = RW_HEADS * HEAD_DIM
D_MIX = NA_WIDTH + RW_WIDTH
NA_KH = 8
NA_KW = 16
DECAY_LORA = 64
AAA_LORA = 64
SHORT_CONV = 3
RMS_EPS = 1e-6
GN_EPS = 64e-5

O_NA_K = 0
O_NA_V = O_NA_K + NA_WIDTH
O_RW_K = O_NA_V + NA_WIDTH
RW_PREP_W = 2 * RW_WIDTH + 2 * DECAY_LORA + 2 * AAA_LORA
O_CTX_END = O_RW_K + RW_PREP_W
O_RW_R = O_CTX_END
O_CONV_END = O_RW_R + RW_WIDTH
O_NA_Q = O_CONV_END
O_NA_G = O_NA_Q + NA_WIDTH
O_RW_G = O_NA_G + NA_WIDTH
D_IN = O_RW_G + RW_WIDTH
CONV_W = O_CONV_END - O_RW_K

kernel_name = 'hybrid_na_rwkv7_dit_block'


def _rmsnorm(x, g):
    xf = x.astype(jnp.float32)
    y = xf * lax.rsqrt(jnp.mean(xf * xf, axis=-1, keepdims=True) + RMS_EPS)
    return (y * g).astype(x.dtype)


def _modulation(cvec, w_mod, b_mod):
    m = jax.nn.silu(cvec) @ w_mod + b_mod
    return jnp.split(m, 3, axis=-1)


def _short_conv(u, w):
    up = jnp.pad(u, ((0, 0), (1, 1), (0, 0)))
    return up[:, :-2] * w[0] + up[:, 1:-1] * w[1] + up[:, 2:] * w[2]


def _na_heads(t):
    B, T, _ = t.shape
    return jnp.transpose(t.reshape(B, T, NA_HEADS, HEAD_DIM), (0, 2, 1, 3))


def _merge(t):
    B, H, T, N = t.shape
    return jnp.transpose(t, (0, 2, 1, 3)).reshape(B, T, H * N)


def _rw_heads(t):
    return t.reshape(t.shape[:-1] + (RW_HEADS, HEAD_DIM))


def _neighborhood_attention(q, k, v, kc, vc, rpb):
    B, H, T, N = q.shape
    rows = T // GRID_W
    kh = min(NA_KH, rows)
    qg = (q * HEAD_DIM ** -0.5).reshape(B, H, rows, GRID_W, N)
    kg = k.reshape(B, H, rows, GRID_W, N)
    vg = v.reshape(B, H, rows, GRID_W, N)
    col = jnp.arange(GRID_W)
    col_idx = jnp.clip(col - NA_KW // 2, 0, GRID_W - NA_KW)[:, None] + jnp.arange(NA_KW)[None, :]
    col_off = col_idx - col[:, None] + (NA_KW - 1)

    def row_block(i):
        r0 = jnp.clip(i - kh // 2, 0, rows - kh)
        kw = jnp.take(lax.dynamic_slice_in_dim(kg, r0, kh, axis=2), col_idx, axis=3)
        vw = jnp.take(lax.dynamic_slice_in_dim(vg, r0, kh, axis=2), col_idx, axis=3)
        qi = lax.dynamic_index_in_dim(qg, i, axis=2, keepdims=False)
        row_off = r0 + jnp.arange(kh) - i + (NA_KH - 1)
        bias = rpb[:, row_off][:, :, col_off]
        s_loc = jnp.einsum('bhqd,bhrqcd->bhqrc', qi, kw) + jnp.transpose(bias, (0, 2, 1, 3))
        s_ctx = jnp.einsum('bhqd,bhld->bhql', qi, kc)
        s = jnp.concatenate([s_loc.reshape(B, H, GRID_W, kh * NA_KW), s_ctx], axis=-1).astype(jnp.float32)
        pr = jax.nn.softmax(s, axis=-1).astype(v.dtype)
        p_loc = pr[..., :kh * NA_KW].reshape(B, H, GRID_W, kh, NA_KW)
        return (jnp.einsum('bhqrc,bhrqcd->bhqd', p_loc, vw)
                + jnp.einsum('bhql,bhld->bhqd', pr[..., kh * NA_KW:], vc))

    out = lax.map(row_block, jnp.arange(rows))
    return jnp.transpose(out, (1, 2, 0, 3, 4)).reshape(B, H, T, N)


def _dense_attention(q, k, v):
    s = jnp.einsum('bhqd,bhkd->bhqk', q, k).astype(jnp.float32) * HEAD_DIM ** -0.5
    pr = jax.nn.softmax(s, axis=-1).astype(v.dtype)
    return jnp.einsum('bhqk,bhkd->bhqd', pr, v)


def _rwkv_prep(u, w0, w2, a0, a2, k_k, k_a):
    B, T, _ = u.shape
    k, v, wd, ad = jnp.split(u, [RW_WIDTH, 2 * RW_WIDTH, 2 * RW_WIDTH + 2 * DECAY_LORA], axis=-1)
    wd = wd.reshape(B, T, 2, DECAY_LORA)
    ad = ad.reshape(B, T, 2, AAA_LORA)
    wlog = -jax.nn.softplus(-(w0 + jnp.einsum('btdr,drc->btdc', jnp.tanh(wd), w2))) - 0.5
    decay = jnp.exp(-jnp.exp(wlog.astype(jnp.float32)))
    a = jax.nn.sigmoid(a0 + jnp.einsum('btdr,drc->btdc', ad, a2))
    kk = _rw_heads(k * k_k).astype(jnp.float32)
    kk = kk * lax.rsqrt(jnp.maximum(jnp.sum(kk * kk, axis=-1, keepdims=True), 1e-24))
    kd = k[:, :, None] * (1.0 + (a - 1.0) * k_a)
    b = kk[:, :, None] * _rw_heads(a)
    return _rw_heads(v), _rw_heads(decay), _rw_heads(kd), kk, b


def _rwkv7_scan(s0, decay, k, kk, b, v, r, reverse):
    tm = lambda t: jnp.moveaxis(t.astype(jnp.float32), 1, 0)
    xs = (tm(decay), tm(k), tm(kk), tm(b), tm(v)) + (() if r is None else (tm(r),))

    def step(s, inp):
        w_t, k_t, kk_t, b_t, v_t = inp[:5]
        sa = jnp.einsum('bhij,bhj->bhi', s, kk_t)
        s = s * w_t[:, :, None, :] - sa[..., None] * b_t[:, :, None, :] + v_t[..., None] * k_t[:, :, None, :]
        y = None if r is None else jnp.einsum('bhij,bhj->bhi', s, inp[5])
        return s, y

    s, ys = lax.scan(step, s0, xs, reverse=reverse)
    return s, (None if r is None else jnp.moveaxis(ys, 0, 1).astype(v.dtype))


def _rwkv_readout(y, r, kd, v, r_k, gn_w, gn_b):
    B, T = y.shape[:2]
    yf = y.astype(jnp.float32)
    mu = jnp.mean(yf, axis=-1, keepdims=True)
    var = jnp.mean(jnp.square(yf - mu), axis=-1, keepdims=True)
    yn = ((yf - mu) * lax.rsqrt(var + GN_EPS)).astype(y.dtype).reshape(B, T, RW_WIDTH) * gn_w + gn_b
    bonus = jnp.sum(jnp.sum(r[:, :, None] * kd * r_k, axis=-1, keepdims=True) * v[:, :, None], axis=2)
    return yn + bonus.reshape(B, T, RW_WIDTH)


def _layer(h, hc, c, c_ctx, w_mod, b_mod, norm_g, w_in, conv_w, decay_w0, decay_w2,
           aaa_a0, aaa_a2, k_k, k_a, r_k, gn_w, gn_b, na_rpb, w_out, update_ctx):
    B = h.shape[0]
    shift, scale, gate = _modulation(c, w_mod, b_mod)
    shift_c, scale_c, gate_c = _modulation(c_ctx, w_mod, b_mod)
    xn = _rmsnorm(h, norm_g) * (1.0 + scale[:, None]) + shift[:, None]
    xc = _rmsnorm(hc, norm_g) * (1.0 + scale_c) + shift_c
    c_end = D_IN if update_ctx else O_CTX_END
    rc_end = O_CONV_END if update_ctx else O_CTX_END
    p = xn @ w_in
    pc = xc @ w_in[:, :c_end]
    rw_args = (decay_w0, decay_w2, aaa_a0, aaa_a2, k_k, k_a)

    q = _na_heads(p[..., O_NA_Q:O_NA_Q + NA_WIDTH])
    k = _na_heads(p[..., O_NA_K:O_NA_K + NA_WIDTH])
    v = _na_heads(p[..., O_NA_V:O_NA_V + NA_WIDTH])
    kc = _na_heads(pc[..., O_NA_K:O_NA_K + NA_WIDTH])
    vc = _na_heads(pc[..., O_NA_V:O_NA_V + NA_WIDTH])
    na = _merge(_neighborhood_attention(q, k, v, kc, vc, na_rpb)) * jax.nn.silu(p[..., O_NA_G:O_NA_G + NA_WIDTH])

    u = _short_conv(p[..., O_RW_K:O_CONV_END], conv_w)
    uc = _short_conv(pc[..., O_RW_K:rc_end], conv_w[:, :rc_end - O_RW_K])
    v_r, dec, kd, kk, bb = _rwkv_prep(u[..., :RW_PREP_W], *rw_args)
    r_r = _rw_heads(u[..., RW_PREP_W:])
    vc_r, dec_c, kd_c, kk_c, bb_c = _rwkv_prep(uc[..., :RW_PREP_W], *rw_args)
    rc_r = _rw_heads(uc[..., RW_PREP_W:]) if update_ctx else None
    s0 = jnp.zeros((B, RW_HEADS, HEAD_DIM, HEAD_DIM), jnp.float32)
    ys, ycs = [], []
    for d, rev in ((0, False), (1, True)):
        s_ctx, y_ctx = _rwkv7_scan(s0, dec_c[:, :, d], kd_c[:, :, d], kk_c, bb_c[:, :, d], vc_r, rc_r, rev)
        _, y = _rwkv7_scan(s_ctx, dec[:, :, d], kd[:, :, d], kk, bb[:, :, d], v_r, r_r, rev)
        ys.append(y)
        ycs.append(y_ctx)
    rw = _rwkv_readout(ys[0] + ys[1], r_r, kd, v_r, r_k, gn_w, gn_b) * jax.nn.silu(p[..., O_RW_G:O_RW_G + RW_WIDTH])

    h = h + gate[:, None] * (jnp.concatenate([na, rw], axis=-1) @ w_out)
    if update_ctx:
        qc = _na_heads(pc[..., O_NA_Q:O_NA_Q + NA_WIDTH])
        na_c = _merge(_dense_attention(qc, kc, vc)) * jax.nn.silu(pc[..., O_NA_G:O_NA_G + NA_WIDTH])
        rw_c = _rwkv_readout(ycs[0] + ycs[1], rc_r, kd_c, vc_r, r_k, gn_w, gn_b) * jax.nn.silu(pc[..., O_RW_G:O_RW_G + RW_WIDTH])
        hc = hc + gate_c * (jnp.concatenate([na_c, rw_c], axis=-1) @ w_out)
    return h, hc


def setup_inputs(seed: int = 0) -> dict:
    key = jax.random.key(seed)
    ks = jax.random.split(key, 24)
    f32 = jnp.float32
    nrm = lambda kk, shape, s: s * jax.random.normal(kk, shape, f32)
    x = nrm(ks[0], (BATCH, SEQ, D_MODEL), 1.0)
    c = nrm(ks[1], (BATCH, D_MODEL), 1.0)
    ctx = nrm(ks[2], (BATCH, CTX_LEN, D_MODEL), 1.0)
    c_ctx = nrm(ks[3], (D_MODEL,), 1.0)
    w_mod = nrm(ks[4], (DEPTH, D_MODEL, 3 * D_MODEL), 0.5 * D_MODEL ** -0.5)
    b_mod = nrm(ks[5], (DEPTH, 3 * D_MODEL), 0.01)
    norm_g = 1.0 + nrm(ks[6], (DEPTH, D_MODEL), 0.02)
    w_in = nrm(ks[7], (DEPTH, D_MODEL, D_IN), D_MODEL ** -0.5)
    taps = jnp.array([0.25, 1.0, 0.25], f32)[None, :, None]
    conv_w = taps + nrm(ks[8], (DEPTH, SHORT_CONV, CONV_W), 0.05)
    decay_w0 = jax.random.uniform(ks[9], (DEPTH, 2, RW_WIDTH), f32, -6.0, 0.0)
    decay_w2 = nrm(ks[10], (DEPTH, 2, DECAY_LORA, RW_WIDTH), 0.5 * DECAY_LORA ** -0.5)
    aaa_a0 = nrm(ks[11], (DEPTH, 2, RW_WIDTH), 0.1)
    aaa_a2 = nrm(ks[12], (DEPTH, 2, AAA_LORA, RW_WIDTH), 0.5 * AAA_LORA ** -0.5)
    k_k = 0.85 + nrm(ks[13], (DEPTH, RW_WIDTH), 0.02)
    k_a = 1.0 + nrm(ks[14], (DEPTH, RW_WIDTH), 0.02)
    r_k = nrm(ks[15], (DEPTH, RW_HEADS, HEAD_DIM), 0.1)
    gn_w = 1.0 + nrm(ks[16], (DEPTH, RW_WIDTH), 0.02)
    gn_b = nrm(ks[17], (DEPTH, RW_WIDTH), 0.01)
    na_rpb = nrm(ks[18], (DEPTH, NA_HEADS, 2 * NA_KH - 1, 2 * NA_KW - 1), 0.1)
    w_out = nrm(ks[19], (DEPTH, D_MIX, D_MODEL), D_MIX ** -0.5)
    final_g = 1.0 + nrm(ks[20], (D_MODEL,), 0.02)
    return {'x': x, 'c': c, 'ctx': ctx, 'c_ctx': c_ctx, 'w_mod': w_mod, 'b_mod': b_mod,
            'norm_g': norm_g, 'w_in': w_in, 'conv_w': conv_w, 'decay_w0': decay_w0,
            'decay_w2': decay_w2, 'aaa_a0': aaa_a0, 'aaa_a2': aaa_a2, 'k_k': k_k, 'k_a': k_a,
            'r_k': r_k, 'gn_w': gn_w, 'gn_b': gn_b, 'na_rpb': na_rpb, 'w_out': w_out,
            'final_g': final_g}


def reference(x, c, ctx, c_ctx, w_mod, b_mod, norm_g, w_in, conv_w, decay_w0, decay_w2,
              aaa_a0, aaa_a2, k_k, k_a, r_k, gn_w, gn_b, na_rpb, w_out, final_g):
    h, hc = x, ctx
    for l in range(DEPTH):
        h, hc = _layer(h, hc, c, c_ctx, w_mod[l], b_mod[l], norm_g[l], w_in[l], conv_w[l],
                       decay_w0[l], decay_w2[l], aaa_a0[l], aaa_a2[l], k_k[l], k_a[l], r_k[l],
                       gn_w[l], gn_b[l], na_rpb[l], w_out[l], update_ctx=(l < DEPTH - 1))
    return _rmsnorm(h, final_g)
```

```python
import functools

import numpy as np
import jax
import jax.numpy as jnp
from jax import lax
from jax.experimental import pallas as pl
from jax.experimental.pallas import tpu as pltpu

F32 = jnp.float32
BF16 = jnp.bfloat16

HEAD_DIM = 64
LANES = 128
NA_HEADS = 8
RW_HEADS = 8
WIDTH = NA_HEADS * HEAD_DIM
N_PAIRS = WIDTH // LANES
GRID_W = 64
NA_KH = 8
NA_KW = 16
LORA = 64
CHUNK = 64
TILE = 256
RMS_EPS = 1e-6
GN_EPS = 64e-5
NEG_BIG = -1e30
VMEM_LIMIT = 56 * 1024 * 1024

O_NA_K = 0
O_NA_V = WIDTH
O_RW_K = 2 * WIDTH
CONV_W = 3 * WIDTH + 4 * LORA
O_NA_Q = O_RW_K + CONV_W
D_IN = O_NA_Q + 3 * WIDTH


def _dot(a, b):
    return jnp.dot(a, b, preferred_element_type=F32)


def _dot_nt(a, b):
    return lax.dot_general(a, b, (((1,), (1,)), ((), ())), preferred_element_type=F32)


def _dot_tn(a, b):
    return lax.dot_general(a, b, (((0,), (0,)), ((), ())), preferred_element_type=F32)


def _split_dot(m, x, terms):
    acc = None
    rem = x
    for _ in range(terms):
        piece = rem.astype(BF16)
        part = _dot(m, piece)
        acc = part if acc is None else acc + part
        rem = rem - piece.astype(F32)
    return acc


def _split_dot_right(x, m, terms):
    acc = None
    rem = x
    for _ in range(terms):
        piece = rem.astype(BF16)
        part = _dot(piece, m)
        acc = part if acc is None else acc + part
        rem = rem - piece.astype(F32)
    return acc


def _sigmoid(z):
    return 1.0 / (1.0 + jnp.exp(-z))


def _params(sem):
    return pltpu.CompilerParams(dimension_semantics=sem, vmem_limit_bytes=VMEM_LIMIT)


def _mod_kernel(c_ref, w_ref, b_ref, o_ref):
    cv = c_ref[...]
    o_ref[...] = _dot(cv * _sigmoid(cv), w_ref[...]) + b_ref[...]


def _modulation(cvecs, w_mod, b_mod):
    rows, d = cvecs.shape
    n_out = w_mod.shape[1]
    blk = d
    return pl.pallas_call(
        _mod_kernel,
        grid=(n_out // blk,),
        in_specs=[pl.BlockSpec((rows, d), lambda j: (0, 0)),
                  pl.BlockSpec((d, blk), lambda j: (0, j)),
                  pl.BlockSpec((1, blk), lambda j: (0, j))],
        out_specs=pl.BlockSpec((rows, blk), lambda j: (0, j)),
        out_shape=jax.ShapeDtypeStruct((rows, n_out), F32),
        compiler_params=_params(("arbitrary",)),
        name="modulation",
    )(cvecs, w_mod, b_mod.reshape(1, n_out))


_PROJ_COLS = 256


def _proj_kernel(n_ctx_tiles, x_ref, ctx_ref, sh_ref, sc_ref, shc_ref, scc_ref, g_ref, w_ref,
                 kv_ref, cv_ref, qg_ref):
    is_ctx = pl.program_id(1) < n_ctx_tiles
    xin = jnp.where(is_ctx, ctx_ref[0], x_ref[0])
    sh = jnp.where(is_ctx, shc_ref[...], sh_ref[0])
    sc = jnp.where(is_ctx, scc_ref[...], sc_ref[0])
    ms = jnp.mean(xin * xin, axis=-1, keepdims=True)
    xn = (xin * lax.rsqrt(ms + RMS_EPS) * g_ref[...]) * (1.0 + sc) + sh
    xb = xn.astype(BF16)

    def emit(out_ref, col0, width, scale_cols=0):
        for j in range(0, width, _PROJ_COLS):
            res = _dot(xb, w_ref[:, col0 + j:col0 + j + _PROJ_COLS])
            if j < scale_cols:
                res = res * (HEAD_DIM ** -0.5)
            out_ref[0, :, j:j + _PROJ_COLS] = res.astype(out_ref.dtype)

    emit(kv_ref, O_NA_K, 2 * WIDTH)
    emit(cv_ref, O_RW_K, CONV_W)
    emit(qg_ref, O_NA_Q, 3 * WIDTH, scale_cols=WIDTH)


def _input_projection(x, ctx, shift, scale, shift_c, scale_c, norm_g, w_in_bf16):
    b, t, d = x.shape
    l = ctx.shape[1]
    n_ctx_tiles = l // TILE
    n_tiles = n_ctx_tiles + t // TILE
    t_tot = l + t
    lat = lambda i, j: (i, jnp.maximum(j - n_ctx_tiles, 0), 0)
    cidx = lambda i, j: (i, jnp.minimum(j, n_ctx_tiles - 1), 0)
    per_b = lambda i, j: (i, 0, 0)
    fixed2 = lambda i, j: (0, 0)
    out_idx = lambda i, j: (i, j, 0)
    return pl.pallas_call(
        functools.partial(_proj_kernel, n_ctx_tiles),
        grid=(b, n_tiles),
        in_specs=[pl.BlockSpec((1, TILE, d), lat),
                  pl.BlockSpec((1, TILE, d), cidx),
                  pl.BlockSpec((1, 1, d), per_b),
                  pl.BlockSpec((1, 1, d), per_b),
                  pl.BlockSpec((1, d), fixed2),
                  pl.BlockSpec((1, d), fixed2),
                  pl.BlockSpec((1, d), fixed2),
                  pl.BlockSpec((d, D_IN), fixed2)],
        out_specs=[pl.BlockSpec((1, TILE, 2 * WIDTH), out_idx),
                   pl.BlockSpec((1, TILE, CONV_W), out_idx),
                   pl.BlockSpec((1, TILE, 3 * WIDTH), out_idx)],
        out_shape=[jax.ShapeDtypeStruct((b, t_tot, 2 * WIDTH), BF16),
                   jax.ShapeDtypeStruct((b, t_tot, CONV_W), F32),
                   jax.ShapeDtypeStruct((b, t_tot, 3 * WIDTH), BF16)],
        compiler_params=_params(("parallel", "arbitrary")),
        name="input_projection",
    )(x, ctx, shift, scale, shift_c, scale_c, norm_g, w_in_bf16)


def _prep_kernel(n_ctx_tiles, cv_ref, prev_ref, next_ref, cw_ref, w0_ref, w2_ref, a0_ref, a2_ref,
                 kk_w_ref, ka_ref, rk_ref, seg_ref, tri_ref, blk_ref,
                 at0, kt0, bt0, rt0, kh0, bh0, ec0, at1, kt1, bt1, rt1, kh1, bh1, ec1, v_ref, bonus_ref):
    tile = pl.program_id(1)
    n_tiles = pl.num_programs(1)
    p = cv_ref[0]
    tm = p.shape[0]
    first = jnp.logical_or(tile == 0, tile == n_ctx_tiles)
    last = jnp.logical_or(tile == n_ctx_tiles - 1, tile == n_tiles - 1)
    prev_row = jnp.where(first, 0.0, prev_ref[0, 7:8, :])
    next_row = jnp.where(last, 0.0, next_ref[0, 0:1, :])
    row = lax.broadcasted_iota(jnp.int32, (tm, 1), 0)
    p_m1 = jnp.where(row == 0, prev_row, pltpu.roll(p, 1, axis=0))
    p_p1 = jnp.where(row == tm - 1, next_row, pltpu.roll(p, tm - 1, axis=0))
    u = p_m1 * cw_ref[0:1, :] + p * cw_ref[1:2, :] + p_p1 * cw_ref[2:3, :]

    k = u[:, 0:WIDTH]
    v = u[:, WIDTH:2 * WIDTH]
    wd = jnp.tanh(u[:, 2 * WIDTH:2 * WIDTH + 2 * LORA]).astype(BF16)
    ad = u[:, 2 * WIDTH + 2 * LORA:2 * WIDTH + 4 * LORA].astype(BF16)
    r = u[:, 2 * WIDTH + 4 * LORA:]
    seg = seg_ref[...]

    kk = k * kk_w_ref[...]
    kk = kk * lax.rsqrt(jnp.maximum(_split_dot_right(kk * kk, seg, 2), 1e-24))
    v_ref[0] = v.astype(BF16)
    blk = blk_ref[...]

    bonus = None
    outs = ((at0, kt0, bt0, rt0, kh0, bh0, ec0), (at1, kt1, bt1, rt1, kh1, bh1, ec1))
    for d, (at, kt, bt, rt, kh, bh, ec) in enumerate(outs):
        lw = -np.exp(-0.5).astype(np.float32) * _sigmoid(w0_ref[d:d + 1, :] + _dot(wd, w2_ref[d]))
        a = _sigmoid(a0_ref[d:d + 1, :] + _dot(ad, a2_ref[d]))
        kd = k * (1.0 + (a - 1.0) * ka_ref[...])
        bb = kk * a
        rkd = _split_dot_right(r * kd * rk_ref[...], seg, 2)
        bonus = rkd if bonus is None else bonus + rkd
        cs = _split_dot(tri_ref[d], lw, 3)
        tot = _split_dot(blk, lw, 3)
        e_inv = jnp.exp(-cs)
        e_rest = jnp.exp(tot - cs)
        at[0] = (kk * jnp.exp(cs - lw)).astype(BF16)
        kt[0] = (kd * e_inv).astype(BF16)
        bt[0] = (bb * e_inv).astype(BF16)
        rt[0] = (r * jnp.exp(cs)).astype(BF16)
        kh[0] = (kd * e_rest).astype(BF16)
        bh[0] = (bb * e_rest).astype(BF16)
        e_tot = jnp.exp(tot)
        for c in range(tm // CHUNK):
            ec[0, c] = e_tot[c * CHUNK:c * CHUNK + 1, :]
    bonus_ref[0] = bonus * v


def _rwkv_prep(cv, n_ctx_tiles, conv_w, w0, w2cat, a0, a2cat, k_k, k_a, r_k, seg, tri, blk):
    b, t_tot, w = cv.shape
    n_tiles = t_tot // TILE
    rows8 = TILE // 8
    main = lambda i, j: (i, j, 0)
    prev = lambda i, j: (i, jnp.maximum(j * rows8 - 1, 0), 0)
    nxt = lambda i, j: (i, jnp.minimum((j + 1) * rows8, t_tot // 8 - 1), 0)
    fixed2 = lambda i, j: (0, 0)
    fixed3 = lambda i, j: (0, 0, 0)
    tok = pl.BlockSpec((1, TILE, WIDTH), main)
    ecs = pl.BlockSpec((1, TILE // CHUNK, 1, WIDTH), lambda i, j: (i, j, 0, 0))
    tok_shape = jax.ShapeDtypeStruct((b, t_tot, WIDTH), BF16)
    ec_shape = jax.ShapeDtypeStruct((b, t_tot // CHUNK, 1, WIDTH), F32)
    per_dir_specs = [tok] * 6 + [ecs]
    per_dir_shapes = [tok_shape] * 6 + [ec_shape]
    return pl.pallas_call(
        functools.partial(_prep_kernel, n_ctx_tiles),
        grid=(b, n_tiles),
        in_specs=[pl.BlockSpec((1, TILE, w), main),
                  pl.BlockSpec((1, 8, w), prev),
                  pl.BlockSpec((1, 8, w), nxt),
                  pl.BlockSpec((3, w), fixed2),
                  pl.BlockSpec((2, WIDTH), fixed2),
                  pl.BlockSpec((2, 2 * LORA, WIDTH), fixed3),
                  pl.BlockSpec((2, WIDTH), fixed2),
                  pl.BlockSpec((2, 2 * LORA, WIDTH), fixed3),
                  pl.BlockSpec((1, WIDTH), fixed2),
                  pl.BlockSpec((1, WIDTH), fixed2),
                  pl.BlockSpec((1, WIDTH), fixed2),
                  pl.BlockSpec((WIDTH, WIDTH), fixed2),
                  pl.BlockSpec((2, TILE, TILE), fixed3),
                  pl.BlockSpec((TILE, TILE), fixed2)],
        out_specs=per_dir_specs * 2 + [tok, tok],
        out_shape=per_dir_shapes * 2 + [tok_shape, jax.ShapeDtypeStruct((b, t_tot, WIDTH), F32)],
        compiler_params=_params(("parallel", "arbitrary")),
        name="rwkv_prep",
    )(cv, cv, cv, conv_w, w0, w2cat, a0, a2cat, k_k, k_a, r_k, seg, tri, blk)


_INV_SQUARINGS = 5


def _stack(xp, lo):
    zero = jnp.zeros_like(xp)
    return jnp.concatenate([jnp.where(lo, xp, zero), jnp.where(lo, zero, xp)], axis=0)


def _scan_unit(rev, a_t, k_t, b_t, r_t, k_h, b_h, v, e_c, s_bd, masks):
    lo, strict_f, incl_f, strict_r, incl_r, eye = masks
    strict, incl = (strict_r, incl_r) if rev else (strict_f, incl_f)
    ast, kst, bst, rst, khst, bhst, vst = (_stack(t, lo) for t in (a_t, k_t, b_t, r_t, k_h, b_h, v))
    n2 = 2 * CHUNK
    scores = _dot_nt(jnp.concatenate([ast, rst], axis=0), jnp.concatenate([kst, bst], axis=0))
    ak = jnp.where(strict, scores[:n2, :n2], 0.0).astype(BF16)
    nm = jnp.where(strict, -scores[:n2, n2:], 0.0)
    rk = jnp.where(incl, scores[n2:, :n2], 0.0).astype(BF16)
    rb = jnp.where(incl, scores[n2:, n2:], 0.0).astype(BF16)
    tinv = jnp.where(eye, 1.0, 0.0) + nm
    pw = nm
    for _ in range(_INV_SQUARINGS):
        pwb = pw.astype(BF16)
        pw = _dot(pwb, pwb)
        tinv = tinv + _dot(tinv.astype(BF16), pw.astype(BF16))
    tinv_b = tinv.astype(BF16)
    g = _dot(tinv_b, ast).astype(BF16)
    u0 = _dot(tinv_b, _dot(ak, vst).astype(BF16)).astype(BF16)
    q = (rst.astype(F32) - _dot(rb, g)).astype(BF16)
    y0 = _dot(rk, vst) - _dot(rb, u0)
    m = (jnp.where(eye, e_c, 0.0) - _dot_tn(bhst, g)).astype(BF16)
    dd = _dot_tn(khst, vst) - _dot_tn(bhst, u0)
    sb = s_bd.astype(BF16)
    yst = _dot(q, sb) + y0
    s_new = _dot(m, sb) + dd
    y = jnp.where(lo, yst[:CHUNK], yst[CHUNK:])
    return y, s_new


def _scan_kernel(at0, kt0, bt0, rt0, kh0, bh0, ec0, v0, at1, kt1, bt1, rt1, kh1, bh1, ec1, v1,
                 y0_ref, y1_ref, s_ref):
    @pl.when(pl.program_id(1) == 0)
    def _():
        s_ref[...] = jnp.zeros_like(s_ref)

    n2 = 2 * CHUNK
    ri = lax.broadcasted_iota(jnp.int32, (n2, n2), 0)
    ci = lax.broadcasted_iota(jnp.int32, (n2, n2), 1)
    shift = CHUNK.bit_length() - 1
    same = (ri >> shift) == (ci >> shift)
    tr = ri & (CHUNK - 1)
    tc = ci & (CHUNK - 1)
    lo = lax.broadcasted_iota(jnp.int32, (CHUNK, LANES), 1) < HEAD_DIM
    masks = (lo,
             jnp.logical_and(same, tr > tc), jnp.logical_and(same, tr >= tc),
             jnp.logical_and(same, tr < tc), jnp.logical_and(same, tr <= tc),
             ri == ci)
    dirs = ((at0, kt0, bt0, rt0, kh0, bh0, ec0, v0, y0_ref), (at1, kt1, bt1, rt1, kh1, bh1, ec1, v1, y1_ref))
    for d, (at, kt, bt, rt, kh, bh, ec, vv, y_ref) in enumerate(dirs):
        for pr in range(N_PAIRS):
            sl = slice(pr * LANES, (pr + 1) * LANES)
            y, s_new = _scan_unit(d == 1, at[0, :, sl], kt[0, :, sl], bt[0, :, sl], rt[0, :, sl],
                                  kh[0, :, sl], bh[0, :, sl], vv[0, :, sl], ec[0, 0, :, sl],
                                  s_ref[d, pr], masks)
            s_ref[d, pr] = s_new
            y_ref[0, :, sl] = y


def _rwkv_scan(prep, n_ctx_chunks):
    (at0, kt0, bt0, rt0, kh0, bh0, ec0, at1, kt1, bt1, rt1, kh1, bh1, ec1, v, _) = prep
    b, t_tot, _ = v.shape
    n_chunks = t_tot // CHUNK

    def rev_chunk(g):
        return jnp.where(g < n_ctx_chunks, n_ctx_chunks - 1 - g, n_chunks - 1 + n_ctx_chunks - g)

    fwd = lambda i, g: (i, g, 0)
    rev = lambda i, g: (i, rev_chunk(g), 0)
    fwd4 = lambda i, g: (i, g, 0, 0)
    rev4 = lambda i, g: (i, rev_chunk(g), 0, 0)
    tokf = pl.BlockSpec((1, CHUNK, WIDTH), fwd)
    tokr = pl.BlockSpec((1, CHUNK, WIDTH), rev)
    ecf = pl.BlockSpec((1, 1, 1, WIDTH), fwd4)
    ecr = pl.BlockSpec((1, 1, 1, WIDTH), rev4)
    y_shape = jax.ShapeDtypeStruct((b, t_tot, WIDTH), F32)
    return pl.pallas_call(
        _scan_kernel,
        grid=(b, n_chunks),
        in_specs=[tokf] * 6 + [ecf, tokf] + [tokr] * 6 + [ecr, tokr],
        out_specs=[tokf, tokr],
        out_shape=[y_shape, y_shape],
        scratch_shapes=[pltpu.VMEM((2, N_PAIRS, 2 * HEAD_DIM, 2 * HEAD_DIM), F32)],
        compiler_params=_params(("parallel", "arbitrary")),
        name="rwkv_scan",
    )(at0, kt0, bt0, rt0, kh0, bh0, ec0, v, at1, kt1, bt1, rt1, kh1, bh1, ec1, v)


def _na_kernel(ctx_len, q_ref, k_ref, v_ref, bias_ref, o_ref):
    i = pl.program_id(2)
    n_rows = pl.num_programs(2)
    r0 = jnp.clip(i - NA_KH // 2, 0, n_rows - NA_KH)
    delta = i - r0
    win = NA_KH * GRID_W
    start = pl.multiple_of(ctx_len + r0 * GRID_W, GRID_W)
    kw = k_ref[0, pl.ds(start, win), :]
    vw = v_ref[0, pl.ds(start, win), :]
    kc = k_ref[0, 0:ctx_len, :]
    vc = v_ref[0, 0:ctx_len, :]
    lo = lax.broadcasted_iota(jnp.int32, (GRID_W, LANES), 1) < HEAD_DIM
    qst = _stack(q_ref[0], lo)
    s_loc = _dot_nt(qst, kw) + bias_ref[delta, 0]
    s_ctx = _dot_nt(qst, kc)
    mx = jnp.maximum(jnp.max(s_loc, axis=-1, keepdims=True), jnp.max(s_ctx, axis=-1, keepdims=True))
    p_loc = jnp.exp(s_loc - mx)
    p_ctx = jnp.exp(s_ctx - mx)
    den = jnp.sum(p_loc, axis=-1, keepdims=True) + jnp.sum(p_ctx, axis=-1, keepdims=True)
    o = (_dot(p_loc.astype(BF16), vw) + _dot(p_ctx.astype(BF16), vc)) / den
    o_ref[0] = jnp.where(lo, o[:GRID_W], o[GRID_W:]).astype(o_ref.dtype)


def _neighborhood_attention(qg, kv, bias, ctx_len):
    b, t_tot, _ = kv.shape
    t = t_tot - ctx_len
    n_rows = t // GRID_W
    ctx_blocks = ctx_len // GRID_W
    return pl.pallas_call(
        functools.partial(_na_kernel, ctx_len),
        grid=(b, N_PAIRS, n_rows),
        in_specs=[pl.BlockSpec((1, GRID_W, LANES), lambda bi, pr, i: (bi, ctx_blocks + i, pr)),
                  pl.BlockSpec((1, t_tot, LANES), lambda bi, pr, i: (bi, 0, pr)),
                  pl.BlockSpec((1, t_tot, LANES), lambda bi, pr, i: (bi, 0, N_PAIRS + pr)),
                  pl.BlockSpec((NA_KH, 1, 2 * GRID_W, NA_KH * GRID_W), lambda bi, pr, i: (0, pr, 0, 0))],
        out_specs=pl.BlockSpec((1, GRID_W, LANES), lambda bi, pr, i: (bi, i, pr)),
        out_shape=jax.ShapeDtypeStruct((b, t, WIDTH), BF16),
        compiler_params=_params(("parallel", "parallel", "arbitrary")),
        name="neighborhood_attention",
    )(qg, kv, kv, bias)


def _na_bias_table(na_rpb):
    col = np.arange(GRID_W)
    c0 = np.clip(col - NA_KW // 2, 0, GRID_W - NA_KW)
    valid = (col[None, :] >= c0[:, None]) & (col[None, :] < c0[:, None] + NA_KW)
    col_off = np.clip(col[None, :] - col[:, None] + NA_KW - 1, 0, 2 * NA_KW - 2)
    row_off = np.arange(NA_KH)[None, :] - np.arange(NA_KH)[:, None] + NA_KH - 1
    tbl = na_rpb[:, row_off][:, :, :, col_off]
    tbl = jnp.where(valid[None, None, None], tbl, NEG_BIG)
    tbl = jnp.transpose(tbl, (1, 0, 3, 2, 4))
    return tbl.reshape(NA_KH, N_PAIRS, 2 * GRID_W, NA_KH * GRID_W).astype(F32)


def _out_kernel(yf_ref, yr_ref, bonus_ref, na_ref, nag_ref, rwg_ref, x_ref, gate_ref, wo_ref,
                gnw_ref, gnb_ref, fg_ref, seg_ref, o_ref):
    seg = seg_ref[...]
    y = yf_ref[0] + yr_ref[0]
    inv_n = 1.0 / HEAD_DIM
    mu = _split_dot_right(y, seg, 3) * inv_n
    dev = y - mu
    var = _split_dot_right(dev * dev, seg, 3) * inv_n
    yn = dev * lax.rsqrt(var + GN_EPS) * gnw_ref[...] + gnb_ref[...]
    rwg = rwg_ref[0].astype(F32)
    nag = nag_ref[0].astype(F32)
    rw = (yn + bonus_ref[0]) * (rwg * _sigmoid(rwg))
    na = na_ref[0].astype(F32) * (nag * _sigmoid(nag))
    mix = jnp.concatenate([na, rw], axis=-1).astype(BF16)
    h = x_ref[0] + gate_ref[0] * _dot(mix, wo_ref[...])
    ms = jnp.mean(h * h, axis=-1, keepdims=True)
    o_ref[0] = h * lax.rsqrt(ms + RMS_EPS) * fg_ref[...]


def _readout_project(yf, yr, bonus, na, qg, x, gate, w_out_bf16, gn_w, gn_b, final_g, seg, n_ctx_tiles):
    b, t, d = x.shape
    lat = lambda i, j: (i, j, 0)
    off = lambda i, j: (i, j + n_ctx_tiles, 0)
    fixed2 = lambda i, j: (0, 0)
    tok_off = pl.BlockSpec((1, TILE, WIDTH), off)
    return pl.pallas_call(
        _out_kernel,
        grid=(b, t // TILE),
        in_specs=[tok_off, tok_off, tok_off,
                  pl.BlockSpec((1, TILE, WIDTH), lat),
                  pl.BlockSpec((1, TILE, WIDTH), lambda i, j: (i, j + n_ctx_tiles, 1)),
                  pl.BlockSpec((1, TILE, WIDTH), lambda i, j: (i, j + n_ctx_tiles, 2)),
                  pl.BlockSpec((1, TILE, d), lat),
                  pl.BlockSpec((1, 1, d), lambda i, j: (i, 0, 0)),
                  pl.BlockSpec((2 * WIDTH, d), fixed2),
                  pl.BlockSpec((1, WIDTH), fixed2),
                  pl.BlockSpec((1, WIDTH), fixed2),
                  pl.BlockSpec((1, d), fixed2),
                  pl.BlockSpec((WIDTH, WIDTH), fixed2)],
        out_specs=pl.BlockSpec((1, TILE, d), lat),
        out_shape=jax.ShapeDtypeStruct((b, t, d), F32),
        compiler_params=_params(("parallel", "arbitrary")),
        name="readout_project",
    )(yf, yr, bonus, na, qg, qg, x, gate, w_out_bf16, gn_w, gn_b, final_g, seg)


def _chunk_matrices():
    idx = np.arange(TILE)
    same = (idx[:, None] // CHUNK) == (idx[None, :] // CHUNK)
    tri_f = same & (idx[:, None] >= idx[None, :])
    tri_r = same & (idx[:, None] <= idx[None, :])
    tri = jnp.asarray(np.stack([tri_f, tri_r]).astype(np.float32), dtype=BF16)
    blk = jnp.asarray(same.astype(np.float32), dtype=BF16)
    hid = np.arange(WIDTH) // HEAD_DIM
    seg = jnp.asarray((hid[:, None] == hid[None, :]).astype(np.float32), dtype=BF16)
    return tri, blk, seg


def _lora_cat(w2):
    z = jnp.zeros_like(w2[0])
    return jnp.stack([jnp.concatenate([w2[0], z], axis=0), jnp.concatenate([z, w2[1]], axis=0)]).astype(BF16)


def kernel(x, c, ctx, c_ctx, w_mod, b_mod, norm_g, w_in, conv_w, decay_w0, decay_w2, aaa_a0, aaa_a2,
           k_k, k_a, r_k, gn_w, gn_b, na_rpb, w_out, final_g):
    depth = w_mod.shape[0]
    assert depth == 1, "single-layer block: the context stream is never updated"
    b, t, d = x.shape
    l = ctx.shape[1]
    assert l == TILE and t % TILE == 0 and t // GRID_W >= NA_KH and w_in.shape[2] == D_IN
    n_ctx_tiles = l // TILE

    rows = -(-(b + 1) // 8) * 8
    cvecs = jnp.zeros((rows, d), F32).at[:b].set(c).at[b].set(c_ctx)
    mod = _modulation(cvecs, w_mod[0], b_mod[0])
    shift, scale, gate = mod[:, :d], mod[:, d:2 * d], mod[:, 2 * d:]

    kv, cv, qg = _input_projection(
        x, ctx, shift[:b].reshape(b, 1, d), scale[:b].reshape(b, 1, d), shift[b:b + 1], scale[b:b + 1],
        norm_g[0].reshape(1, d), w_in[0].astype(BF16))

    tri, blk, seg = _chunk_matrices()
    prep = _rwkv_prep(cv, n_ctx_tiles, conv_w[0], decay_w0[0], _lora_cat(decay_w2[0]), aaa_a0[0],
                      _lora_cat(aaa_a2[0]), k_k[0].reshape(1, WIDTH), k_a[0].reshape(1, WIDTH),
                      r_k[0].reshape(1, WIDTH), seg, tri, blk)
    yf, yr = _rwkv_scan(prep, l // CHUNK)
    na = _neighborhood_attention(qg, kv, _na_bias_table(na_rpb[0]), l)
    return _readout_project(yf, yr, prep[-1], na, qg, x, gate[:b].reshape(b, 1, d), w_out[0].astype(BF16),
                            gn_w[0].reshape(1, WIDTH), gn_b[0].reshape(1, WIDTH), final_g.reshape(1, d), seg,
                            n_ctx_tiles)
```

```python
import functools

import numpy as np
import jax
import jax.numpy as jnp
from jax import lax
from jax.experimental import pallas as pl
from jax.experimental.pallas import tpu as pltpu

F32 = jnp.float32
BF16 = jnp.bfloat16

HEAD_DIM = 64
LANES = 128
NA_HEADS = 8
RW_HEADS = 8
WIDTH = NA_HEADS * HEAD_DIM
N_PAIRS = WIDTH // LANES
GRID_W = 64
NA_KH = 8
NA_KW = 16
LORA = 64
CHUNK = 64
TILE = 256
RMS_EPS = 1e-6
GN_EPS = 64e-5
NEG_BIG = -1e30
VMEM_LIMIT = 56 * 1024 * 1024

O_NA_K = 0
O_NA_V = WIDTH
O_RW_K = 2 * WIDTH
CONV_W = 3 * WIDTH + 4 * LORA
O_NA_Q = O_RW_K + CONV_W
D_IN = O_NA_Q + 3 * WIDTH


def _dot(a, b):
    return jnp.dot(a, b, preferred_element_type=F32)


def _dot_nt(a, b):
    return lax.dot_general(a, b, (((1,), (1,)), ((), ())), preferred_element_type=F32)


def _dot_tn(a, b):
    return lax.dot_general(a, b, (((0,), (0,)), ((), ())), preferred_element_type=F32)


def _split_dot(m, x, terms):
    acc = None
    rem = x
    for _ in range(terms):
        piece = rem.astype(BF16)
        part = _dot(m, piece)
        acc = part if acc is None else acc + part
        rem = rem - piece.astype(F32)
    return acc


def _split_dot_right(x, m, terms):
    acc = None
    rem = x
    for _ in range(terms):
        piece = rem.astype(BF16)
        part = _dot(piece, m)
        acc = part if acc is None else acc + part
        rem = rem - piece.astype(F32)
    return acc


def _sigmoid(z):
    return 1.0 / (1.0 + jnp.exp(-z))


def _params(sem):
    return pltpu.CompilerParams(dimension_semantics=sem, vmem_limit_bytes=VMEM_LIMIT)


def _mod_kernel(c_ref, w_ref, b_ref, o_ref):
    cv = c_ref[...]
    o_ref[...] = _dot(cv * _sigmoid(cv), w_ref[...]) + b_ref[...]


def _modulation(cvecs, w_mod, b_mod):
    rows, d = cvecs.shape
    n_out = w_mod.shape[1]
    blk = d
    return pl.pallas_call(
        _mod_kernel,
        grid=(n_out // blk,),
        in_specs=[pl.BlockSpec((rows, d), lambda j: (0, 0)),
                  pl.BlockSpec((d, blk), lambda j: (0, j)),
                  pl.BlockSpec((1, blk), lambda j: (0, j))],
        out_specs=pl.BlockSpec((rows, blk), lambda j: (0, j)),
        out_shape=jax.ShapeDtypeStruct((rows, n_out), F32),
        compiler_params=_params(("arbitrary",)),
        name="modulation",
    )(cvecs, w_mod, b_mod.reshape(1, n_out))


_PROJ_COLS = 256


def _proj_kernel(n_ctx_tiles, x_ref, ctx_ref, sh_ref, sc_ref, shc_ref, scc_ref, g_ref, w_ref,
                 kv_ref, cv_ref, qg_ref):
    is_ctx = pl.program_id(1) < n_ctx_tiles
    xin = jnp.where(is_ctx, ctx_ref[0], x_ref[0])
    sh = jnp.where(is_ctx, shc_ref[...], sh_ref[0])
    sc = jnp.where(is_ctx, scc_ref[...], sc_ref[0])
    ms = jnp.mean(xin * xin, axis=-1, keepdims=True)
    xn = (xin * lax.rsqrt(ms + RMS_EPS) * g_ref[...]) * (1.0 + sc) + sh
    xb = xn.astype(BF16)

    def emit(out_ref, col0, width, scale_cols=0):
        for j in range(0, width, _PROJ_COLS):
            res = _dot(xb, w_ref[:, col0 + j:col0 + j + _PROJ_COLS])
            if j < scale_cols:
                res = res * (HEAD_DIM ** -0.5)
            out_ref[0, :, j:j + _PROJ_COLS] = res.astype(out_ref.dtype)

    emit(kv_ref, O_NA_K, 2 * WIDTH)
    emit(cv_ref, O_RW_K, CONV_W)
    emit(qg_ref, O_NA_Q, 3 * WIDTH, scale_cols=WIDTH)


def _input_projection(x, ctx, shift, scale, shift_c, scale_c, norm_g, w_in_bf16):
    b, t, d = x.shape
    l = ctx.shape[1]
    n_ctx_tiles = l // TILE
    n_tiles = n_ctx_tiles + t // TILE
    t_tot = l + t
    lat = lambda i, j: (i, jnp.maximum(j - n_ctx_tiles, 0), 0)
    cidx = lambda i, j: (i, jnp.minimum(j, n_ctx_tiles - 1), 0)
    per_b = lambda i, j: (i, 0, 0)
    fixed2 = lambda i, j: (0, 0)
    out_idx = lambda i, j: (i, j, 0)
    return pl.pallas_call(
        functools.partial(_proj_kernel, n_ctx_tiles),
        grid=(b, n_tiles),
        in_specs=[pl.BlockSpec((1, TILE, d), lat),
                  pl.BlockSpec((1, TILE, d), cidx),
                  pl.BlockSpec((1, 1, d), per_b),
                  pl.BlockSpec((1, 1, d), per_b),
                  pl.BlockSpec((1, d), fixed2),
                  pl.BlockSpec((1, d), fixed2),
                  pl.BlockSpec((1, d), fixed2),
                  pl.BlockSpec((d, D_IN), fixed2)],
        out_specs=[pl.BlockSpec((1, TILE, 2 * WIDTH), out_idx),
                   pl.BlockSpec((1, TILE, CONV_W), out_idx),
                   pl.BlockSpec((1, TILE, 3 * WIDTH), out_idx)],
        out_shape=[jax.ShapeDtypeStruct((b, t_tot, 2 * WIDTH), BF16),
                   jax.ShapeDtypeStruct((b, t_tot, CONV_W), F32),
                   jax.ShapeDtypeStruct((b, t_tot, 3 * WIDTH), BF16)],
        compiler_params=_params(("parallel", "arbitrary")),
        name="input_projection",
    )(x, ctx, shift, scale, shift_c, scale_c, norm_g, w_in_bf16)


def _prep_kernel(n_ctx_tiles, cv_ref, prev_ref, next_ref, cw_ref, w0_ref, w2_ref, a0_ref, a2_ref,
                 kk_w_ref, ka_ref, rk_ref, seg_ref, tri_ref, blk_ref,
                 at0, kt0, bt0, rt0, kh0, bh0, ec0, at1, kt1, bt1, rt1, kh1, bh1, ec1, v_ref, bonus_ref):
    tile = pl.program_id(1)
    n_tiles = pl.num_programs(1)
    p = cv_ref[0]
    tm = p.shape[0]
    first = jnp.logical_or(tile == 0, tile == n_ctx_tiles)
    last = jnp.logical_or(tile == n_ctx_tiles - 1, tile == n_tiles - 1)
    prev_row = jnp.where(first, 0.0, prev_ref[0, 7:8, :])
    next_row = jnp.where(last, 0.0, next_ref[0, 0:1, :])
    row = lax.broadcasted_iota(jnp.int32, (tm, 1), 0)
    p_m1 = jnp.where(row == 0, prev_row, pltpu.roll(p, 1, axis=0))
    p_p1 = jnp.where(row == tm - 1, next_row, pltpu.roll(p, tm - 1, axis=0))
    u = p_m1 * cw_ref[0:1, :] + p * cw_ref[1:2, :] + p_p1 * cw_ref[2:3, :]

    k = u[:, 0:WIDTH]
    v = u[:, WIDTH:2 * WIDTH]
    wd = jnp.tanh(u[:, 2 * WIDTH:2 * WIDTH + 2 * LORA]).astype(BF16)
    ad = u[:, 2 * WIDTH + 2 * LORA:2 * WIDTH + 4 * LORA].astype(BF16)
    r = u[:, 2 * WIDTH + 4 * LORA:]
    seg = seg_ref[...]

    kk = k * kk_w_ref[...]
    kk = kk * lax.rsqrt(jnp.maximum(_split_dot_right(kk * kk, seg, 2), 1e-24))
    v_ref[0] = v.astype(BF16)
    blk = blk_ref[...]

    bonus = None
    outs = ((at0, kt0, bt0, rt0, kh0, bh0, ec0), (at1, kt1, bt1, rt1, kh1, bh1, ec1))
    for d, (at, kt, bt, rt, kh, bh, ec) in enumerate(outs):
        lw = -np.exp(-0.5).astype(np.float32) * _sigmoid(w0_ref[d:d + 1, :] + _dot(wd, w2_ref[d]))
        a = _sigmoid(a0_ref[d:d + 1, :] + _dot(ad, a2_ref[d]))
        kd = k * (1.0 + (a - 1.0) * ka_ref[...])
        bb = kk * a
        rkd = _split_dot_right(r * kd * rk_ref[...], seg, 2)
        bonus = rkd if bonus is None else bonus + rkd
        cs = _split_dot(tri_ref[d], lw, 3)
        tot = _split_dot(blk, lw, 3)
        e_inv = jnp.exp(-cs)
        e_rest = jnp.exp(tot - cs)
        at[0] = (kk * jnp.exp(cs - lw)).astype(BF16)
        kt[0] = (kd * e_inv).astype(BF16)
        bt[0] = (bb * e_inv).astype(BF16)
        rt[0] = (r * jnp.exp(cs)).astype(BF16)
        kh[0] = (kd * e_rest).astype(BF16)
        bh[0] = (bb * e_rest).astype(BF16)
        e_tot = jnp.exp(tot)
        for c in range(tm // CHUNK):
            ec[0, c] = e_tot[c * CHUNK:c * CHUNK + 1, :]
    bonus_ref[0] = bonus * v


def _rwkv_prep(cv, n_ctx_tiles, conv_w, w0, w2cat, a0, a2cat, k_k, k_a, r_k, seg, tri, blk):
    b, t_tot, w = cv.shape
    n_tiles = t_tot // TILE
    rows8 = TILE // 8
    main = lambda i, j: (i, j, 0)
    prev = lambda i, j: (i, jnp.maximum(j * rows8 - 1, 0), 0)
    nxt = lambda i, j: (i, jnp.minimum((j + 1) * rows8, t_tot // 8 - 1), 0)
    fixed2 = lambda i, j: (0, 0)
    fixed3 = lambda i, j: (0, 0, 0)
    tok = pl.BlockSpec((1, TILE, WIDTH), main)
    ecs = pl.BlockSpec((1, TILE // CHUNK, 1, WIDTH), lambda i, j: (i, j, 0, 0))
    tok_shape = jax.ShapeDtypeStruct((b, t_tot, WIDTH), BF16)
    ec_shape = jax.ShapeDtypeStruct((b, t_tot // CHUNK, 1, WIDTH), F32)
    per_dir_specs = [tok] * 6 + [ecs]
    per_dir_shapes = [tok_shape] * 6 + [ec_shape]
    return pl.pallas_call(
        functools.partial(_prep_kernel, n_ctx_tiles),
        grid=(b, n_tiles),
        in_specs=[pl.BlockSpec((1, TILE, w), main),
                  pl.BlockSpec((1, 8, w), prev),
                  pl.BlockSpec((1, 8, w), nxt),
                  pl.BlockSpec((3, w), fixed2),
                  pl.BlockSpec((2, WIDTH), fixed2),
                  pl.BlockSpec((2, 2 * LORA, WIDTH), fixed3),
                  pl.BlockSpec((2, WIDTH), fixed2),
                  pl.BlockSpec((2, 2 * LORA, WIDTH), fixed3),
                  pl.BlockSpec((1, WIDTH), fixed2),
                  pl.BlockSpec((1, WIDTH), fixed2),
                  pl.BlockSpec((1, WIDTH), fixed2),
                  pl.BlockSpec((WIDTH, WIDTH), fixed2),
                  pl.BlockSpec((2, TILE, TILE), fixed3),
                  pl.BlockSpec((TILE, TILE), fixed2)],
        out_specs=per_dir_specs * 2 + [tok, tok],
        out_shape=per_dir_shapes * 2 + [tok_shape, jax.ShapeDtypeStruct((b, t_tot, WIDTH), F32)],
        compiler_params=_params(("parallel", "arbitrary")),
        name="rwkv_prep",
    )(cv, cv, cv, conv_w, w0, w2cat, a0, a2cat, k_k, k_a, r_k, seg, tri, blk)


_INV_SQUARINGS = 5


def _stack(xp, lo):
    zero = jnp.zeros_like(xp)
    return jnp.concatenate([jnp.where(lo, xp, zero), jnp.where(lo, zero, xp)], axis=0)


def _scan_units(units, masks):
    lo, strict_f, incl_f, strict_r, incl_r, eye = masks
    n2 = 2 * CHUNK
    cat = jnp.concatenate
    st = [{k: _stack(u[k], lo) for k in ("a_t", "k_t", "b_t", "r_t", "k_h", "b_h", "v")} for u in units]
    strict = [strict_r if u["rev"] else strict_f for u in units]
    incl = [incl_r if u["rev"] else incl_f for u in units]
    scores = [_dot_nt(cat([s["a_t"], s["r_t"]], axis=0), cat([s["k_t"], s["b_t"]], axis=0)) for s in st]
    ak = [jnp.where(mk, sc[:n2, :n2], 0.0).astype(BF16) for mk, sc in zip(strict, scores)]
    nm = [jnp.where(mk, -sc[:n2, n2:], 0.0) for mk, sc in zip(strict, scores)]
    rkb = [jnp.where(mk, sc[n2:, :], 0.0).astype(BF16) for mk, sc in zip(incl, scores)]
    w1 = [_dot(a, s["v"]).astype(BF16) for a, s in zip(ak, st)]
    eye_f = jnp.where(eye, 1.0, 0.0)
    pw = [_dot(x.astype(BF16), x.astype(BF16)) for x in nm]
    tinv = [eye_f + x for x in nm]
    for _ in range(_INV_SQUARINGS - 1):
        pwb = [x.astype(BF16) for x in pw]
        prod = [_dot(cat([pb, t.astype(BF16)], axis=0), pb) for pb, t in zip(pwb, tinv)]
        pw = [x[:n2] for x in prod]
        tinv = [t + x[n2:] for t, x in zip(tinv, prod)]
    tinv = [t + _dot(t.astype(BF16), x.astype(BF16)) for t, x in zip(tinv, pw)]
    gu = [_dot(t.astype(BF16), cat([s["a_t"], w], axis=1)).astype(BF16) for t, s, w in zip(tinv, st, w1)]
    g = [x[:, :n2] for x in gu]
    vu = [cat([s["v"], -x[:, n2:]], axis=0) for s, x in zip(st, gu)]
    q = [(s["r_t"].astype(F32) - _dot(r[:, n2:], gg)).astype(BF16) for s, r, gg in zip(st, rkb, g)]
    y0 = [_dot(r, x) for r, x in zip(rkb, vu)]
    m = [(jnp.where(eye, u["e_c"], 0.0) - _dot_tn(s["b_h"], gg)).astype(BF16) for u, s, gg in zip(units, st, g)]
    dd = [_dot_tn(cat([s["k_h"], s["b_h"]], axis=0), x) for s, x in zip(st, vu)]
    qs = [_dot(cat([qq, mm], axis=0), u["s"].astype(BF16)) for qq, mm, u in zip(q, m, units)]
    out = []
    for x, yy, d2 in zip(qs, y0, dd):
        yst = x[:n2] + yy
        out.append((jnp.where(lo, yst[:CHUNK], yst[CHUNK:]), x[n2:] + d2))
    return out


def _scan_kernel(at0, kt0, bt0, rt0, kh0, bh0, ec0, v0, at1, kt1, bt1, rt1, kh1, bh1, ec1, v1,
                 y0_ref, y1_ref, s_ref):
    @pl.when(pl.program_id(1) == 0)
    def _():
        s_ref[...] = jnp.zeros_like(s_ref)

    n2 = 2 * CHUNK
    ri = lax.broadcasted_iota(jnp.int32, (n2, n2), 0)
    ci = lax.broadcasted_iota(jnp.int32, (n2, n2), 1)
    shift = CHUNK.bit_length() - 1
    same = (ri >> shift) == (ci >> shift)
    tr = ri & (CHUNK - 1)
    tc = ci & (CHUNK - 1)
    lo = lax.broadcasted_iota(jnp.int32, (CHUNK, LANES), 1) < HEAD_DIM
    ri2 = lax.broadcasted_iota(jnp.int32, (n2, 2 * n2), 0)
    ci2 = lax.broadcasted_iota(jnp.int32, (n2, 2 * n2), 1) & (n2 - 1)
    same2 = (ri2 >> shift) == (ci2 >> shift)
    tr2 = ri2 & (CHUNK - 1)
    tc2 = ci2 & (CHUNK - 1)
    masks = (lo,
             jnp.logical_and(same, tr > tc), jnp.logical_and(same2, tr2 >= tc2),
             jnp.logical_and(same, tr < tc), jnp.logical_and(same2, tr2 <= tc2),
             ri == ci)
    dirs = ((at0, kt0, bt0, rt0, kh0, bh0, ec0, v0, y0_ref), (at1, kt1, bt1, rt1, kh1, bh1, ec1, v1, y1_ref))
    units = []
    for d, (at, kt, bt, rt, kh, bh, ec, vv, _) in enumerate(dirs):
        for pr in range(N_PAIRS):
            sl = slice(pr * LANES, (pr + 1) * LANES)
            units.append(dict(rev=d == 1, a_t=at[0, :, sl], k_t=kt[0, :, sl], b_t=bt[0, :, sl], r_t=rt[0, :, sl],
                              k_h=kh[0, :, sl], b_h=bh[0, :, sl], v=vv[0, :, sl], e_c=ec[0, 0, :, sl],
                              s=s_ref[d, pr]))
    results = _scan_units(units, masks)
    for idx, (y, s_new) in enumerate(results):
        d, pr = divmod(idx, N_PAIRS)
        s_ref[d, pr] = s_new
        dirs[d][-1][0, :, pr * LANES:(pr + 1) * LANES] = y


def _rwkv_scan(prep, n_ctx_chunks):
    (at0, kt0, bt0, rt0, kh0, bh0, ec0, at1, kt1, bt1, rt1, kh1, bh1, ec1, v, _) = prep
    b, t_tot, _ = v.shape
    n_chunks = t_tot // CHUNK

    def rev_chunk(g):
        return jnp.where(g < n_ctx_chunks, n_ctx_chunks - 1 - g, n_chunks - 1 + n_ctx_chunks - g)

    fwd = lambda i, g: (i, g, 0)
    rev = lambda i, g: (i, rev_chunk(g), 0)
    fwd4 = lambda i, g: (i, g, 0, 0)
    rev4 = lambda i, g: (i, rev_chunk(g), 0, 0)
    tokf = pl.BlockSpec((1, CHUNK, WIDTH), fwd)
    tokr = pl.BlockSpec((1, CHUNK, WIDTH), rev)
    ecf = pl.BlockSpec((1, 1, 1, WIDTH), fwd4)
    ecr = pl.BlockSpec((1, 1, 1, WIDTH), rev4)
    y_shape = jax.ShapeDtypeStruct((b, t_tot, WIDTH), F32)
    return pl.pallas_call(
        _scan_kernel,
        grid=(b, n_chunks),
        in_specs=[tokf] * 6 + [ecf, tokf] + [tokr] * 6 + [ecr, tokr],
        out_specs=[tokf, tokr],
        out_shape=[y_shape, y_shape],
        scratch_shapes=[pltpu.VMEM((2, N_PAIRS, 2 * HEAD_DIM, 2 * HEAD_DIM), F32)],
        compiler_params=_params(("parallel", "arbitrary")),
        name="rwkv_scan",
    )(at0, kt0, bt0, rt0, kh0, bh0, ec0, v, at1, kt1, bt1, rt1, kh1, bh1, ec1, v)


NA_ROWS = 4


def _na_kernel(ctx_len, n_rows, q_ref, k_ref, v_ref, bias_ref, o_ref):
    step = pl.program_id(2)
    win = NA_KH * GRID_W
    kc = k_ref[0, 0:ctx_len, :]
    vc = v_ref[0, 0:ctx_len, :]
    lo = lax.broadcasted_iota(jnp.int32, (GRID_W, LANES), 1) < HEAD_DIM
    rows = range(NA_ROWS)
    r0 = [jnp.clip(step * NA_ROWS + j - NA_KH // 2, 0, n_rows - NA_KH) for j in rows]
    delta = [step * NA_ROWS + j - r for j, r in zip(rows, r0)]
    start = [pl.multiple_of(ctx_len + r * GRID_W, GRID_W) for r in r0]
    qst = [_stack(q_ref[0, j * GRID_W:(j + 1) * GRID_W, :], lo) for j in rows]
    s_loc = [_dot_nt(q, k_ref[0, pl.ds(s, win), :]) + bias_ref[dl, 0] for q, s, dl in zip(qst, start, delta)]
    s_ctx = [_dot_nt(q, kc) for q in qst]
    mx = [jnp.maximum(jnp.max(a, axis=-1, keepdims=True), jnp.max(c, axis=-1, keepdims=True))
          for a, c in zip(s_loc, s_ctx)]
    p_loc = [jnp.exp(a - m) for a, m in zip(s_loc, mx)]
    p_ctx = [jnp.exp(c - m) for c, m in zip(s_ctx, mx)]
    den = [jnp.sum(a, axis=-1, keepdims=True) + jnp.sum(c, axis=-1, keepdims=True) for a, c in zip(p_loc, p_ctx)]
    o = [(_dot(a.astype(BF16), v_ref[0, pl.ds(s, win), :]) + _dot(c.astype(BF16), vc)) / dn
         for a, c, s, dn in zip(p_loc, p_ctx, start, den)]
    for j, oj in zip(rows, o):
        o_ref[0, j * GRID_W:(j + 1) * GRID_W, :] = jnp.where(lo, oj[:GRID_W], oj[GRID_W:]).astype(o_ref.dtype)


def _neighborhood_attention(qg, kv, bias, ctx_len):
    b, t_tot, _ = kv.shape
    t = t_tot - ctx_len
    n_rows = t // GRID_W
    assert n_rows % NA_ROWS == 0 and ctx_len % (NA_ROWS * GRID_W) == 0
    ctx_blocks = ctx_len // (NA_ROWS * GRID_W)
    blk_rows = NA_ROWS * GRID_W
    return pl.pallas_call(
        functools.partial(_na_kernel, ctx_len, n_rows),
        grid=(b, N_PAIRS, n_rows // NA_ROWS),
        in_specs=[pl.BlockSpec((1, blk_rows, LANES), lambda bi, pr, i: (bi, ctx_blocks + i, pr)),
                  pl.BlockSpec((1, t_tot, LANES), lambda bi, pr, i: (bi, 0, pr)),
                  pl.BlockSpec((1, t_tot, LANES), lambda bi, pr, i: (bi, 0, N_PAIRS + pr)),
                  pl.BlockSpec((NA_KH, 1, 2 * GRID_W, NA_KH * GRID_W), lambda bi, pr, i: (0, pr, 0, 0))],
        out_specs=pl.BlockSpec((1, blk_rows, LANES), lambda bi, pr, i: (bi, i, pr)),
        out_shape=jax.ShapeDtypeStruct((b, t, WIDTH), BF16),
        compiler_params=_params(("parallel", "parallel", "arbitrary")),
        name="neighborhood_attention",
    )(qg, kv, kv, bias)


def _na_bias_table(na_rpb):
    col = np.arange(GRID_W)
    c0 = np.clip(col - NA_KW // 2, 0, GRID_W - NA_KW)
    valid = (col[None, :] >= c0[:, None]) & (col[None, :] < c0[:, None] + NA_KW)
    col_off = np.clip(col[None, :] - col[:, None] + NA_KW - 1, 0, 2 * NA_KW - 2)
    row_off = np.arange(NA_KH)[None, :] - np.arange(NA_KH)[:, None] + NA_KH - 1
    tbl = na_rpb[:, row_off][:, :, :, col_off]
    tbl = jnp.where(valid[None, None, None], tbl, NEG_BIG)
    tbl = jnp.transpose(tbl, (1, 0, 3, 2, 4))
    return tbl.reshape(NA_KH, N_PAIRS, 2 * GRID_W, NA_KH * GRID_W).astype(F32)


def _out_kernel(yf_ref, yr_ref, bonus_ref, na_ref, nag_ref, rwg_ref, x_ref, gate_ref, wo_ref,
                gnw_ref, gnb_ref, fg_ref, seg_ref, o_ref):
    seg = seg_ref[...]
    y = yf_ref[0] + yr_ref[0]
    inv_n = 1.0 / HEAD_DIM
    mu = _split_dot_right(y, seg, 3) * inv_n
    dev = y - mu
    var = _split_dot_right(dev * dev, seg, 3) * inv_n
    yn = dev * lax.rsqrt(var + GN_EPS) * gnw_ref[...] + gnb_ref[...]
    rwg = rwg_ref[0].astype(F32)
    nag = nag_ref[0].astype(F32)
    rw = (yn + bonus_ref[0]) * (rwg * _sigmoid(rwg))
    na = na_ref[0].astype(F32) * (nag * _sigmoid(nag))
    mix = jnp.concatenate([na, rw], axis=-1).astype(BF16)
    h = x_ref[0] + gate_ref[0] * _dot(mix, wo_ref[...])
    ms = jnp.mean(h * h, axis=-1, keepdims=True)
    o_ref[0] = h * lax.rsqrt(ms + RMS_EPS) * fg_ref[...]


def _readout_project(yf, yr, bonus, na, qg, x, gate, w_out_bf16, gn_w, gn_b, final_g, seg, n_ctx_tiles):
    b, t, d = x.shape
    lat = lambda i, j: (i, j, 0)
    off = lambda i, j: (i, j + n_ctx_tiles, 0)
    fixed2 = lambda i, j: (0, 0)
    tok_off = pl.BlockSpec((1, TILE, WIDTH), off)
    return pl.pallas_call(
        _out_kernel,
        grid=(b, t // TILE),
        in_specs=[tok_off, tok_off, tok_off,
                  pl.BlockSpec((1, TILE, WIDTH), lat),
                  pl.BlockSpec((1, TILE, WIDTH), lambda i, j: (i, j + n_ctx_tiles, 1)),
                  pl.BlockSpec((1, TILE, WIDTH), lambda i, j: (i, j + n_ctx_tiles, 2)),
                  pl.BlockSpec((1, TILE, d), lat),
                  pl.BlockSpec((1, 1, d), lambda i, j: (i, 0, 0)),
                  pl.BlockSpec((2 * WIDTH, d), fixed2),
                  pl.BlockSpec((1, WIDTH), fixed2),
                  pl.BlockSpec((1, WIDTH), fixed2),
                  pl.BlockSpec((1, d), fixed2),
                  pl.BlockSpec((WIDTH, WIDTH), fixed2)],
        out_specs=pl.BlockSpec((1, TILE, d), lat),
        out_shape=jax.ShapeDtypeStruct((b, t, d), F32),
        compiler_params=_params(("parallel", "arbitrary")),
        name="readout_project",
    )(yf, yr, bonus, na, qg, qg, x, gate, w_out_bf16, gn_w, gn_b, final_g, seg)


def _chunk_matrices():
    idx = np.arange(TILE)
    same = (idx[:, None] // CHUNK) == (idx[None, :] // CHUNK)
    tri_f = same & (idx[:, None] >= idx[None, :])
    tri_r = same & (idx[:, None] <= idx[None, :])
    tri = jnp.asarray(np.stack([tri_f, tri_r]).astype(np.float32), dtype=BF16)
    blk = jnp.asarray(same.astype(np.float32), dtype=BF16)
    hid = np.arange(WIDTH) // HEAD_DIM
    seg = jnp.asarray((hid[:, None] == hid[None, :]).astype(np.float32), dtype=BF16)
    return tri, blk, seg


def _lora_cat(w2):
    z = jnp.zeros_like(w2[0])
    return jnp.stack([jnp.concatenate([w2[0], z], axis=0), jnp.concatenate([z, w2[1]], axis=0)]).astype(BF16)


def kernel(x, c, ctx, c_ctx, w_mod, b_mod, norm_g, w_in, conv_w, decay_w0, decay_w2, aaa_a0, aaa_a2,
           k_k, k_a, r_k, gn_w, gn_b, na_rpb, w_out, final_g):
    depth = w_mod.shape[0]
    assert depth == 1, "single-layer block: the context stream is never updated"
    b, t, d = x.shape
    l = ctx.shape[1]
    assert l == TILE and t % TILE == 0 and t // GRID_W >= NA_KH and w_in.shape[2] == D_IN
    n_ctx_tiles = l // TILE

    rows = -(-(b + 1) // 8) * 8
    cvecs = jnp.zeros((rows, d), F32).at[:b].set(c).at[b].set(c_ctx)
    mod = _modulation(cvecs, w_mod[0], b_mod[0])
    shift, scale, gate = mod[:, :d], mod[:, d:2 * d], mod[:, 2 * d:]

    kv, cv, qg = _input_projection(
        x, ctx, shift[:b].reshape(b, 1, d), scale[:b].reshape(b, 1, d), shift[b:b + 1], scale[b:b + 1],
        norm_g[0].reshape(1, d), w_in[0].astype(BF16))

    tri, blk, seg = _chunk_matrices()
    prep = _rwkv_prep(cv, n_ctx_tiles, conv_w[0], decay_w0[0], _lora_cat(decay_w2[0]), aaa_a0[0],
                      _lora_cat(aaa_a2[0]), k_k[0].reshape(1, WIDTH), k_a[0].reshape(1, WIDTH),
                      r_k[0].reshape(1, WIDTH), seg, tri, blk)
    yf, yr = _rwkv_scan(prep, l // CHUNK)
    na = _neighborhood_attention(qg, kv, _na_bias_table(na_rpb[0]), l)
    return _readout_project(yf, yr, prep[-1], na, qg, x, gate[:b].reshape(b, 1, d), w_out[0].astype(BF16),
                            gn_w[0].reshape(1, WIDTH), gn_b[0].reshape(1, WIDTH), final_g.reshape(1, d), seg,
                            n_ctx_tiles)
```

```python
import functools

import numpy as np
import jax
import jax.numpy as jnp
from jax import lax
from jax.experimental import pallas as pl
from jax.experimental.pallas import tpu as pltpu

F32 = jnp.float32
BF16 = jnp.bfloat16

HEAD_DIM = 64
LANES = 128
BF16_ROWS = 16
NA_HEADS = 8
RW_HEADS = 8
WIDTH = NA_HEADS * HEAD_DIM
N_PAIRS = WIDTH // LANES
GRID_W = 64
NA_KH = 8
NA_KW = 16
LORA = 64
CHUNK = 64
TILE = 256
RMS_EPS = 1e-6
GN_EPS = 64e-5
NEG_BIG = -1e30
LOG2E = 1.4426950408889634
VMEM_LIMIT = 56 * 1024 * 1024

O_NA_K = 0
O_NA_V = WIDTH
O_RW_K = 2 * WIDTH
CONV_W = 3 * WIDTH + 4 * LORA
O_NA_Q = O_RW_K + CONV_W
D_IN = O_NA_Q + 3 * WIDTH


def _dot(a, b):
    return jnp.dot(a, b, preferred_element_type=F32)


def _dot_nt(a, b):
    return lax.dot_general(a, b, (((1,), (1,)), ((), ())), preferred_element_type=F32)


def _dot_tn(a, b):
    return lax.dot_general(a, b, (((0,), (0,)), ((), ())), preferred_element_type=F32)


def _split_dot(m, x, terms):
    acc = None
    rem = x
    for _ in range(terms):
        piece = rem.astype(BF16)
        part = _dot(m, piece)
        acc = part if acc is None else acc + part
        rem = rem - piece.astype(F32)
    return acc


def _split_dot_right(x, m, terms):
    acc = None
    rem = x
    for _ in range(terms):
        piece = rem.astype(BF16)
        part = _dot(piece, m)
        acc = part if acc is None else acc + part
        rem = rem - piece.astype(F32)
    return acc


def _sigmoid(z):
    return 1.0 / (1.0 + jnp.exp2(z * (-LOG2E)))


def _params(sem):
    return pltpu.CompilerParams(dimension_semantics=sem, vmem_limit_bytes=VMEM_LIMIT)


def _mod_kernel(c_ref, w_ref, b_ref, o_ref):
    cv = c_ref[...]
    o_ref[...] = _dot(cv * _sigmoid(cv), w_ref[...]) + b_ref[...]


def _modulation(cvecs, w_mod, b_mod):
    rows, d = cvecs.shape
    n_out = w_mod.shape[1]
    blk = d
    return pl.pallas_call(
        _mod_kernel,
        grid=(n_out // blk,),
        in_specs=[pl.BlockSpec((rows, d), lambda j: (0, 0)),
                  pl.BlockSpec((d, blk), lambda j: (0, j)),
                  pl.BlockSpec((1, blk), lambda j: (0, j))],
        out_specs=pl.BlockSpec((rows, blk), lambda j: (0, j)),
        out_shape=jax.ShapeDtypeStruct((rows, n_out), F32),
        compiler_params=_params(("arbitrary",)),
        name="modulation",
    )(cvecs, w_mod, b_mod.reshape(1, n_out))


_PROJ_COLS = 256


def _proj_kernel(n_ctx_tiles, x_ref, ctx_ref, sh_ref, sc_ref, shc_ref, scc_ref, g_ref, w_ref,
                 kv_ref, cv_ref, qg_ref):
    is_ctx = pl.program_id(1) < n_ctx_tiles
    xin = jnp.where(is_ctx, ctx_ref[0], x_ref[0])
    sh = jnp.where(is_ctx, shc_ref[...], sh_ref[0])
    sc = jnp.where(is_ctx, scc_ref[...], sc_ref[0])
    ms = jnp.mean(xin * xin, axis=-1, keepdims=True)
    xn = (xin * lax.rsqrt(ms + RMS_EPS) * g_ref[...]) * (1.0 + sc) + sh
    xb = xn.astype(BF16)

    def emit(out_ref, col0, width, scale_cols=0):
        for j in range(0, width, _PROJ_COLS):
            res = _dot(xb, w_ref[:, col0 + j:col0 + j + _PROJ_COLS])
            if j < scale_cols:
                res = res * (HEAD_DIM ** -0.5)
            out_ref[0, :, j:j + _PROJ_COLS] = res.astype(out_ref.dtype)

    emit(kv_ref, O_NA_K, 2 * WIDTH)
    emit(cv_ref, O_RW_K, CONV_W)
    emit(qg_ref, O_NA_Q, 3 * WIDTH, scale_cols=WIDTH)


def _input_projection(x, ctx, shift, scale, shift_c, scale_c, norm_g, w_in_bf16):
    b, t, d = x.shape
    l = ctx.shape[1]
    n_ctx_tiles = l // TILE
    n_tiles = n_ctx_tiles + t // TILE
    t_tot = l + t
    lat = lambda i, j: (i, jnp.maximum(j - n_ctx_tiles, 0), 0)
    cidx = lambda i, j: (i, jnp.minimum(j, n_ctx_tiles - 1), 0)
    per_b = lambda i, j: (i, 0, 0)
    fixed2 = lambda i, j: (0, 0)
    out_idx = lambda i, j: (i, j, 0)
    return pl.pallas_call(
        functools.partial(_proj_kernel, n_ctx_tiles),
        grid=(b, n_tiles),
        in_specs=[pl.BlockSpec((1, TILE, d), lat),
                  pl.BlockSpec((1, TILE, d), cidx),
                  pl.BlockSpec((1, 1, d), per_b),
                  pl.BlockSpec((1, 1, d), per_b),
                  pl.BlockSpec((1, d), fixed2),
                  pl.BlockSpec((1, d), fixed2),
                  pl.BlockSpec((1, d), fixed2),
                  pl.BlockSpec((d, D_IN), fixed2)],
        out_specs=[pl.BlockSpec((1, TILE, 2 * WIDTH), out_idx),
                   pl.BlockSpec((1, TILE, CONV_W), out_idx),
                   pl.BlockSpec((1, TILE, 3 * WIDTH), lat)],
        out_shape=[jax.ShapeDtypeStruct((b, t_tot, 2 * WIDTH), BF16),
                   jax.ShapeDtypeStruct((b, t_tot, CONV_W), BF16),
                   jax.ShapeDtypeStruct((b, t, 3 * WIDTH), BF16)],
        compiler_params=_params(("parallel", "arbitrary")),
        name="input_projection",
    )(x, ctx, shift, scale, shift_c, scale_c, norm_g, w_in_bf16)


def _prep_kernel(n_ctx_tiles, cv_ref, prev_ref, next_ref, cw_ref, w0_ref, w2_ref, a0_ref, a2_ref,
                 kk_w_ref, ka_ref, rk_ref, seg_ref, tri_ref, shift_ref,
                 at0, kt0, bt0, rt0, ec0, at1, kt1, bt1, rt1, ec1, v_ref, bonus_ref):
    tile = pl.program_id(1)
    n_tiles = pl.num_programs(1)
    tm = cv_ref.shape[1]
    first = jnp.logical_or(tile == 0, tile == n_ctx_tiles)
    last = jnp.logical_or(tile == n_ctx_tiles - 1, tile == n_tiles - 1)
    row8 = lax.broadcasted_iota(jnp.int32, (8, 1), 0)

    def conv(lo_col, hi_col):
        cols = slice(lo_col, hi_col)
        pb = cv_ref[0, :, cols]
        w_prev, w_mid, w_next = cw_ref[0:1, cols], cw_ref[1:2, cols], cw_ref[2:3, cols]
        u = _dot(shift_ref[0], pb) * w_prev + pb.astype(F32) * w_mid + _dot(shift_ref[1], pb) * w_next
        prev_row = jnp.where(first, 0.0, prev_ref[0, BF16_ROWS - 1:BF16_ROWS, cols].astype(F32))
        next_row = jnp.where(last, 0.0, next_ref[0, 0:1, cols].astype(F32))
        top = u[0:8] + jnp.where(row8 == 0, prev_row * w_prev, 0.0)
        bot = u[tm - 8:] + jnp.where(row8 == 7, next_row * w_next, 0.0)
        return jnp.concatenate([top, u[8:tm - 8], bot], axis=0)

    k = conv(0, WIDTH)
    v = conv(WIDTH, 2 * WIDTH)
    lora_in = conv(2 * WIDTH, 2 * WIDTH + 4 * LORA)
    wd = jnp.tanh(lora_in[:, :2 * LORA]).astype(BF16)
    ad = lora_in[:, 2 * LORA:].astype(BF16)
    r = conv(2 * WIDTH + 4 * LORA, CONV_W)
    seg = seg_ref[...]

    kk = k * kk_w_ref[...]
    kk = kk * lax.rsqrt(jnp.maximum(_split_dot_right(kk * kk, seg, 2), 1e-24))
    v_ref[0] = v.astype(BF16)
    k_ka = k * ka_ref[...]
    k_rest = k - k_ka
    r_rk = r * rk_ref[...]
    lw_scale = np.float32(-np.exp(-0.5) * LOG2E)

    kd_sum = None
    outs = ((at0, kt0, bt0, rt0, ec0), (at1, kt1, bt1, rt1, ec1))
    for d, (at, kt, bt, rt, ec) in enumerate(outs):
        lw = lw_scale * _sigmoid(w0_ref[d:d + 1, :] + _dot(wd, w2_ref[d]))
        a = _sigmoid(a0_ref[d:d + 1, :] + _dot(ad, a2_ref[d]))
        kd = k_rest + k_ka * a
        bb = kk * a
        kd_sum = kd if kd_sum is None else kd_sum + kd
        cs = _split_dot(tri_ref[d], lw, 2)
        e_inv = jnp.exp2(-cs)
        at[0] = (kk * jnp.exp2(cs - lw)).astype(BF16)
        kt[0] = (kd * e_inv).astype(BF16)
        bt[0] = (bb * e_inv).astype(BF16)
        rt[0] = (r * jnp.exp2(cs)).astype(BF16)
        for c in range(tm // CHUNK):
            end = c * CHUNK + (CHUNK - 1 if d == 0 else 0)
            ec[0, c] = jnp.exp2(cs[end:end + 1, :])
    bonus_ref[0] = _split_dot_right(r_rk * kd_sum, seg, 2) * v


def _rwkv_prep(cv, n_ctx_tiles, conv_w, w0, w2cat, a0, a2cat, k_k, k_a, r_k, seg, tri, shift):
    b, t_tot, w = cv.shape
    n_tiles = t_tot // TILE
    halo = TILE // BF16_ROWS
    main = lambda i, j: (i, j, 0)
    prev = lambda i, j: (i, jnp.maximum(j * halo - 1, 0), 0)
    nxt = lambda i, j: (i, jnp.minimum((j + 1) * halo, t_tot // BF16_ROWS - 1), 0)
    fixed2 = lambda i, j: (0, 0)
    fixed3 = lambda i, j: (0, 0, 0)
    tok = pl.BlockSpec((1, TILE, WIDTH), main)
    ecs = pl.BlockSpec((1, TILE // CHUNK, 1, WIDTH), lambda i, j: (i, j, 0, 0))
    tok_shape = jax.ShapeDtypeStruct((b, t_tot, WIDTH), BF16)
    ec_shape = jax.ShapeDtypeStruct((b, t_tot // CHUNK, 1, WIDTH), F32)
    per_dir_specs = [tok] * 4 + [ecs]
    per_dir_shapes = [tok_shape] * 4 + [ec_shape]
    return pl.pallas_call(
        functools.partial(_prep_kernel, n_ctx_tiles),
        grid=(b, n_tiles),
        in_specs=[pl.BlockSpec((1, TILE, w), main),
                  pl.BlockSpec((1, BF16_ROWS, w), prev),
                  pl.BlockSpec((1, BF16_ROWS, w), nxt),
                  pl.BlockSpec((3, w), fixed2),
                  pl.BlockSpec((2, WIDTH), fixed2),
                  pl.BlockSpec((2, 2 * LORA, WIDTH), fixed3),
                  pl.BlockSpec((2, WIDTH), fixed2),
                  pl.BlockSpec((2, 2 * LORA, WIDTH), fixed3),
                  pl.BlockSpec((1, WIDTH), fixed2),
                  pl.BlockSpec((1, WIDTH), fixed2),
                  pl.BlockSpec((1, WIDTH), fixed2),
                  pl.BlockSpec((WIDTH, WIDTH), fixed2),
                  pl.BlockSpec((2, TILE, TILE), fixed3),
                  pl.BlockSpec((2, TILE, TILE), fixed3)],
        out_specs=per_dir_specs * 2 + [tok, tok],
        out_shape=per_dir_shapes * 2 + [tok_shape, jax.ShapeDtypeStruct((b, t_tot, WIDTH), F32)],
        compiler_params=_params(("parallel", "arbitrary")),
        name="rwkv_prep",
    )(cv, cv, cv, conv_w, w0, w2cat, a0, a2cat, k_k, k_a, r_k, seg, tri, shift)


_INV_SQUARINGS = 5


def _stack(xp, lo):
    zero = jnp.zeros_like(xp)
    return jnp.concatenate([jnp.where(lo, xp, zero), jnp.where(lo, zero, xp)], axis=0)


def _chunk_operators(units, masks):
    lo, strict_f, incl_f, strict_r, incl_r, eye_w, eye_sq, same_head = masks
    cat = jnp.concatenate
    c = CHUNK
    b16 = lambda xs: [x.astype(BF16) for x in xs]
    stack = lambda xs: [_stack(x, lo) for x in xs]
    strict = [strict_r if u["rev"] else strict_f for u in units]
    incl = [incl_r if u["rev"] else incl_f for u in units]
    a_t, k_t, b_t, r_t, v = ([u[n] for u in units] for n in ("a_t", "k_t", "b_t", "r_t", "v"))
    ast, kst, bst, vst = stack(a_t), stack(k_t), stack(b_t), stack(v)
    scores = [_dot_nt(cat([a, r], axis=0), cat([ks, bs], axis=0)) for a, r, ks, bs in zip(a_t, r_t, kst, bst)]
    ak = b16([jnp.where(mk, sc[:c, :LANES], 0.0) for mk, sc in zip(strict, scores)])
    nm = [jnp.where(mk, -sc[:c, LANES:], 0.0) for mk, sc in zip(strict, scores)]
    rkb = b16([jnp.where(mk, sc[c:, :], 0.0) for mk, sc in zip(incl, scores)])
    w1st = stack(b16([_dot(x, vs) for x, vs in zip(ak, vst)]))
    nmb = b16(nm)
    pw = [_dot(x, xs) for x, xs in zip(nmb, stack(nmb))]
    tinv = [eye_w + x for x in nm]
    for _ in range(_INV_SQUARINGS - 1):
        pwb = b16(pw)
        prod = [_dot(cat([pb, t.astype(BF16)], axis=0), ps) for pb, t, ps in zip(pwb, tinv, stack(pwb))]
        pw = [x[:c] for x in prod]
        tinv = [t + x[c:] for t, x in zip(tinv, prod)]
    tinv = [t + _dot(t.astype(BF16), ps) for t, ps in zip(tinv, stack(b16(pw)))]
    gu = b16([_dot(t.astype(BF16), cat([x, w], axis=1)) for t, x, w in zip(tinv, ast, w1st)])
    g = [x[:, :LANES] for x in gu]
    u0 = [x[:, LANES:] for x in gu]
    q = b16([r.astype(F32) - _dot(rb[:, LANES:], gs) for r, rb, gs in zip(r_t, rkb, stack(g))])
    y0 = [_dot(rb, cat([vs, -us], axis=0)) for rb, vs, us in zip(rkb, vst, stack(u0))]
    k_h = b16([x.astype(F32) * u["e_c"] for x, u in zip(k_t, units)])
    b_h = b16([x.astype(F32) * u["e_c"] for x, u in zip(b_t, units)])
    m = b16([jnp.where(same_head, jnp.where(eye_sq, u["e_c"], 0.0) - _dot_tn(bh, gg), 0.0)
             for u, bh, gg in zip(units, b_h, g)])
    dd = [jnp.where(same_head, _dot_tn(cat([kh, bh], axis=0), cat([vv, -uu], axis=0)), 0.0)
          for kh, bh, vv, uu in zip(k_h, b_h, v, u0)]
    return [(cat([qq, mm], axis=0), yy, d2) for qq, mm, yy, d2 in zip(q, m, y0, dd)]


SCAN_CHUNKS = 2


def _scan_kernel(at0, kt0, bt0, rt0, ec0, v0, at1, kt1, bt1, rt1, ec1, v1, y0_ref, y1_ref, s_ref):
    @pl.when(pl.program_id(1) == 0)
    def _():
        s_ref[...] = jnp.zeros_like(s_ref)

    c = CHUNK
    lane_w = lax.broadcasted_iota(jnp.int32, (c, LANES), 1)
    t_w = lax.broadcasted_iota(jnp.int32, (c, LANES), 0)
    s_w = lane_w & (c - 1)
    t_w2 = lax.broadcasted_iota(jnp.int32, (c, 2 * LANES), 0)
    s_w2 = lax.broadcasted_iota(jnp.int32, (c, 2 * LANES), 1) & (c - 1)
    ri = lax.broadcasted_iota(jnp.int32, (LANES, LANES), 0)
    ci = lax.broadcasted_iota(jnp.int32, (LANES, LANES), 1)
    masks = (lane_w < HEAD_DIM,
             t_w > s_w, t_w2 >= s_w2, t_w < s_w, t_w2 <= s_w2,
             jnp.where(t_w == s_w, 1.0, 0.0), ri == ci, (ri // HEAD_DIM) == (ci // HEAD_DIM))
    dirs = ((at0, kt0, bt0, rt0, ec0, v0, y0_ref), (at1, kt1, bt1, rt1, ec1, v1, y1_ref))
    units = []
    where = []
    for j in range(SCAN_CHUNKS):
        for d, (at, kt, bt, rt, ec, vv, y_ref) in enumerate(dirs):
            pos = j if d == 0 else SCAN_CHUNKS - 1 - j
            rows = slice(pos * c, (pos + 1) * c)
            for pr in range(N_PAIRS):
                sl = slice(pr * LANES, (pr + 1) * LANES)
                units.append(dict(rev=d == 1, a_t=at[0, rows, sl], k_t=kt[0, rows, sl], b_t=bt[0, rows, sl],
                                  r_t=rt[0, rows, sl], v=vv[0, rows, sl], e_c=ec[0, pos, :, sl]))
                where.append((d, pr, y_ref, rows, sl))
    ops = _chunk_operators(units, masks)
    n_seq = 2 * N_PAIRS
    state = [s_ref[d, pr] for (d, pr, _, _, _) in where[:n_seq]]
    for j in range(SCAN_CHUNKS):
        step_ops = ops[j * n_seq:(j + 1) * n_seq]
        prods = [_dot(qm, s.astype(BF16)) for (qm, _, _), s in zip(step_ops, state)]
        for x, (_, y0, _), (_, _, y_ref, rows, sl) in zip(prods, step_ops, where[j * n_seq:(j + 1) * n_seq]):
            y_ref[0, rows, sl] = x[:c] + y0
        state = [x[c:] + dd for x, (_, _, dd) in zip(prods, step_ops)]
    for s, (d, pr, _, _, _) in zip(state, where[:n_seq]):
        s_ref[d, pr] = s


def _rwkv_scan(prep, n_ctx_chunks):
    (at0, kt0, bt0, rt0, ec0, at1, kt1, bt1, rt1, ec1, v, _) = prep
    b, t_tot, _ = v.shape
    assert t_tot % (SCAN_CHUNKS * CHUNK) == 0 and n_ctx_chunks % SCAN_CHUNKS == 0
    n_blocks = t_tot // (SCAN_CHUNKS * CHUNK)
    n_ctx_blocks = n_ctx_chunks // SCAN_CHUNKS

    def rev_block(g):
        return jnp.where(g < n_ctx_blocks, n_ctx_blocks - 1 - g, n_blocks - 1 + n_ctx_blocks - g)

    fwd = lambda i, g: (i, g, 0)
    rev = lambda i, g: (i, rev_block(g), 0)
    fwd4 = lambda i, g: (i, g, 0, 0)
    rev4 = lambda i, g: (i, rev_block(g), 0, 0)
    tokf = pl.BlockSpec((1, SCAN_CHUNKS * CHUNK, WIDTH), fwd)
    tokr = pl.BlockSpec((1, SCAN_CHUNKS * CHUNK, WIDTH), rev)
    ecf = pl.BlockSpec((1, SCAN_CHUNKS, 1, WIDTH), fwd4)
    ecr = pl.BlockSpec((1, SCAN_CHUNKS, 1, WIDTH), rev4)
    y_shape = jax.ShapeDtypeStruct((b, t_tot, WIDTH), F32)
    return pl.pallas_call(
        _scan_kernel,
        grid=(b, n_blocks),
        in_specs=[tokf] * 4 + [ecf, tokf] + [tokr] * 4 + [ecr, tokr],
        out_specs=[tokf, tokr],
        out_shape=[y_shape, y_shape],
        scratch_shapes=[pltpu.VMEM((2, N_PAIRS, 2 * HEAD_DIM, 2 * HEAD_DIM), F32)],
        compiler_params=_params(("parallel", "arbitrary")),
        name="rwkv_scan",
    )(at0, kt0, bt0, rt0, ec0, v, at1, kt1, bt1, rt1, ec1, v)


NA_ROWS = 8


def _na_kernel(ctx_len, n_rows, q_ref, k_ref, v_ref, bias_ref, o_ref):
    step = pl.program_id(2)
    win = NA_KH * GRID_W
    kc = k_ref[0, 0:ctx_len, :]
    vc = v_ref[0, 0:ctx_len, :]
    lo = lax.broadcasted_iota(jnp.int32, (GRID_W, LANES), 1) < HEAD_DIM
    rows = range(NA_ROWS)
    r0 = [jnp.clip(step * NA_ROWS + j - NA_KH // 2, 0, n_rows - NA_KH) for j in rows]
    delta = [step * NA_ROWS + j - r for j, r in zip(rows, r0)]
    start = [pl.multiple_of(ctx_len + r * GRID_W, GRID_W) for r in r0]
    qst = [_stack(q_ref[0, j * GRID_W:(j + 1) * GRID_W, :], lo) for j in rows]
    s_loc = [_dot_nt(q, k_ref[0, pl.ds(s, win), :]) + bias_ref[dl, 0] for q, s, dl in zip(qst, start, delta)]
    s_ctx = [_dot_nt(q, kc) for q in qst]
    mx = [jnp.maximum(jnp.max(a, axis=-1, keepdims=True), jnp.max(c, axis=-1, keepdims=True))
          for a, c in zip(s_loc, s_ctx)]
    p_loc = [jnp.exp(a - m) for a, m in zip(s_loc, mx)]
    p_ctx = [jnp.exp(c - m) for c, m in zip(s_ctx, mx)]
    den = [jnp.sum(a, axis=-1, keepdims=True) + jnp.sum(c, axis=-1, keepdims=True) for a, c in zip(p_loc, p_ctx)]
    o = [(_dot(a.astype(BF16), v_ref[0, pl.ds(s, win), :]) + _dot(c.astype(BF16), vc)) / dn
         for a, c, s, dn in zip(p_loc, p_ctx, start, den)]
    for j, oj in zip(rows, o):
        o_ref[0, j * GRID_W:(j + 1) * GRID_W, :] = jnp.where(lo, oj[:GRID_W], oj[GRID_W:]).astype(o_ref.dtype)


def _neighborhood_attention(qg, kv, bias, ctx_len):
    b, t_tot, _ = kv.shape
    t = t_tot - ctx_len
    n_rows = t // GRID_W
    assert n_rows % NA_ROWS == 0
    blk_rows = NA_ROWS * GRID_W
    return pl.pallas_call(
        functools.partial(_na_kernel, ctx_len, n_rows),
        grid=(b, N_PAIRS, n_rows // NA_ROWS),
        in_specs=[pl.BlockSpec((1, blk_rows, LANES), lambda bi, pr, i: (bi, i, pr)),
                  pl.BlockSpec((1, t_tot, LANES), lambda bi, pr, i: (bi, 0, pr)),
                  pl.BlockSpec((1, t_tot, LANES), lambda bi, pr, i: (bi, 0, N_PAIRS + pr)),
                  pl.BlockSpec((NA_KH, 1, 2 * GRID_W, NA_KH * GRID_W), lambda bi, pr, i: (0, pr, 0, 0))],
        out_specs=pl.BlockSpec((1, blk_rows, LANES), lambda bi, pr, i: (bi, i, pr)),
        out_shape=jax.ShapeDtypeStruct((b, t, WIDTH), BF16),
        compiler_params=_params(("parallel", "parallel", "arbitrary")),
        name="neighborhood_attention",
    )(qg, kv, kv, bias)


def _na_bias_table(na_rpb):
    col = np.arange(GRID_W)
    c0 = np.clip(col - NA_KW // 2, 0, GRID_W - NA_KW)
    valid = (col[None, :] >= c0[:, None]) & (col[None, :] < c0[:, None] + NA_KW)
    col_off = np.clip(col[None, :] - col[:, None] + NA_KW - 1, 0, 2 * NA_KW - 2)
    row_off = np.arange(NA_KH)[None, :] - np.arange(NA_KH)[:, None] + NA_KH - 1
    tbl = na_rpb[:, row_off][:, :, :, col_off]
    tbl = jnp.where(valid[None, None, None], tbl, NEG_BIG)
    tbl = jnp.transpose(tbl, (1, 0, 3, 2, 4))
    return tbl.reshape(NA_KH, N_PAIRS, 2 * GRID_W, NA_KH * GRID_W).astype(F32)


def _out_kernel(yf_ref, yr_ref, bonus_ref, na_ref, nag_ref, rwg_ref, x_ref, gate_ref, wo_ref,
                gnw_ref, gnb_ref, fg_ref, seg_ref, o_ref):
    seg = seg_ref[...]
    y = yf_ref[0] + yr_ref[0]
    inv_n = 1.0 / HEAD_DIM
    mu = _split_dot_right(y, seg, 3) * inv_n
    dev = y - mu
    var = _split_dot_right(dev * dev, seg, 3) * inv_n
    yn = dev * lax.rsqrt(var + GN_EPS) * gnw_ref[...] + gnb_ref[...]
    rwg = rwg_ref[0].astype(F32)
    nag = nag_ref[0].astype(F32)
    rw = (yn + bonus_ref[0]) * (rwg * _sigmoid(rwg))
    na = na_ref[0].astype(F32) * (nag * _sigmoid(nag))
    mix = jnp.concatenate([na, rw], axis=-1).astype(BF16)
    h = x_ref[0] + gate_ref[0] * _dot(mix, wo_ref[...])
    ms = jnp.mean(h * h, axis=-1, keepdims=True)
    o_ref[0] = h * lax.rsqrt(ms + RMS_EPS) * fg_ref[...]


def _readout_project(yf, yr, bonus, na, qg, x, gate, w_out_bf16, gn_w, gn_b, final_g, seg, n_ctx_tiles):
    b, t, d = x.shape
    lat = lambda i, j: (i, j, 0)
    off = lambda i, j: (i, j + n_ctx_tiles, 0)
    fixed2 = lambda i, j: (0, 0)
    tok_off = pl.BlockSpec((1, TILE, WIDTH), off)
    return pl.pallas_call(
        _out_kernel,
        grid=(b, t // TILE),
        in_specs=[tok_off, tok_off, tok_off,
                  pl.BlockSpec((1, TILE, WIDTH), lat),
                  pl.BlockSpec((1, TILE, WIDTH), lambda i, j: (i, j, 1)),
                  pl.BlockSpec((1, TILE, WIDTH), lambda i, j: (i, j, 2)),
                  pl.BlockSpec((1, TILE, d), lat),
                  pl.BlockSpec((1, 1, d), lambda i, j: (i, 0, 0)),
                  pl.BlockSpec((2 * WIDTH, d), fixed2),
                  pl.BlockSpec((1, WIDTH), fixed2),
                  pl.BlockSpec((1, WIDTH), fixed2),
                  pl.BlockSpec((1, d), fixed2),
                  pl.BlockSpec((WIDTH, WIDTH), fixed2)],
        out_specs=pl.BlockSpec((1, TILE, d), lat),
        out_shape=jax.ShapeDtypeStruct((b, t, d), F32),
        compiler_params=_params(("parallel", "arbitrary")),
        name="readout_project",
    )(yf, yr, bonus, na, qg, qg, x, gate, w_out_bf16, gn_w, gn_b, final_g, seg)


def _chunk_matrices():
    idx = np.arange(TILE)
    same = (idx[:, None] // CHUNK) == (idx[None, :] // CHUNK)
    tri_f = same & (idx[:, None] >= idx[None, :])
    tri_r = same & (idx[:, None] <= idx[None, :])
    tri = jnp.asarray(np.stack([tri_f, tri_r]).astype(np.float32), dtype=BF16)
    hid = np.arange(WIDTH) // HEAD_DIM
    seg = jnp.asarray((hid[:, None] == hid[None, :]).astype(np.float32), dtype=BF16)
    down = idx[:, None] == idx[None, :] + 1
    shift = jnp.asarray(np.stack([down, down.T]).astype(np.float32), dtype=BF16)
    return tri, seg, shift


def _lora_cat(w2):
    z = jnp.zeros_like(w2[0])
    return jnp.stack([jnp.concatenate([w2[0], z], axis=0), jnp.concatenate([z, w2[1]], axis=0)]).astype(BF16)


def kernel(x, c, ctx, c_ctx, w_mod, b_mod, norm_g, w_in, conv_w, decay_w0, decay_w2, aaa_a0, aaa_a2,
           k_k, k_a, r_k, gn_w, gn_b, na_rpb, w_out, final_g):
    depth = w_mod.shape[0]
    assert depth == 1, "single-layer block: the context stream is never updated"
    b, t, d = x.shape
    l = ctx.shape[1]
    assert l == TILE and t % TILE == 0 and t // GRID_W >= NA_KH and w_in.shape[2] == D_IN
    n_ctx_tiles = l // TILE

    rows = -(-(b + 1) // 8) * 8
    cvecs = jnp.zeros((rows, d), F32).at[:b].set(c).at[b].set(c_ctx)
    mod = _modulation(cvecs, w_mod[0], b_mod[0])
    shift, scale, gate = mod[:, :d], mod[:, d:2 * d], mod[:, 2 * d:]

    kv, cv, qg = _input_projection(
        x, ctx, shift[:b].reshape(b, 1, d), scale[:b].reshape(b, 1, d), shift[b:b + 1], scale[b:b + 1],
        norm_g[0].reshape(1, d), w_in[0].astype(BF16))

    tri, seg, shift = _chunk_matrices()
    prep = _rwkv_prep(cv, n_ctx_tiles, conv_w[0], decay_w0[0], _lora_cat(decay_w2[0]), aaa_a0[0],
                      _lora_cat(aaa_a2[0]), k_k[0].reshape(1, WIDTH), k_a[0].reshape(1, WIDTH),
                      r_k[0].reshape(1, WIDTH), seg, tri, shift)
    yf, yr = _rwkv_scan(prep, l // CHUNK)
    na = _neighborhood_attention(qg, kv, _na_bias_table(na_rpb[0]), l)
    return _readout_project(yf, yr, prep[-1], na, qg, x, gate[:b].reshape(b, 1, d), w_out[0].astype(BF16),
                            gn_w[0].reshape(1, WIDTH), gn_b[0].reshape(1, WIDTH), final_g.reshape(1, d), seg,
                            n_ctx_tiles)
```

```python
import functools

import numpy as np
import jax
import jax.numpy as jnp
from jax import lax
from jax.experimental import pallas as pl
from jax.experimental.pallas import tpu as pltpu

F32 = jnp.float32
BF16 = jnp.bfloat16

HEAD_DIM = 64
LANES = 128
BF16_ROWS = 16
NA_HEADS = 8
RW_HEADS = 8
WIDTH = NA_HEADS * HEAD_DIM
N_PAIRS = WIDTH // LANES
GRID_W = 64
NA_KH = 8
NA_KW = 16
LORA = 64
CHUNK = 64
TILE = 256
RMS_EPS = 1e-6
GN_EPS = 64e-5
NEG_BIG = -1e30
LOG2E = 1.4426950408889634
VMEM_LIMIT = 56 * 1024 * 1024

O_NA_K = 0
O_NA_V = WIDTH
O_RW_K = 2 * WIDTH
CONV_W = 3 * WIDTH + 4 * LORA
O_NA_Q = O_RW_K + CONV_W
D_IN = O_NA_Q + 3 * WIDTH


def _dot(a, b):
    return jnp.dot(a, b, preferred_element_type=F32)


def _dot_nt(a, b):
    return lax.dot_general(a, b, (((1,), (1,)), ((), ())), preferred_element_type=F32)


def _dot_tn(a, b):
    return lax.dot_general(a, b, (((0,), (0,)), ((), ())), preferred_element_type=F32)


def _split_dot(m, x, terms):
    acc = None
    rem = x
    for _ in range(terms):
        piece = rem.astype(BF16)
        part = _dot(m, piece)
        acc = part if acc is None else acc + part
        rem = rem - piece.astype(F32)
    return acc


def _split_dot_right(x, m, terms):
    acc = None
    rem = x
    for _ in range(terms):
        piece = rem.astype(BF16)
        part = _dot(piece, m)
        acc = part if acc is None else acc + part
        rem = rem - piece.astype(F32)
    return acc


def _sigmoid(z):
    return 1.0 / (1.0 + jnp.exp2(z * (-LOG2E)))


def _params(sem):
    return pltpu.CompilerParams(dimension_semantics=sem, vmem_limit_bytes=VMEM_LIMIT)


def _mod_kernel(c_ref, w_ref, b_ref, o_ref):
    cv = c_ref[...]
    o_ref[...] = _dot(cv * _sigmoid(cv), w_ref[...]) + b_ref[...]


def _modulation(cvecs, w_mod, b_mod):
    rows, d = cvecs.shape
    n_out = w_mod.shape[1]
    blk = d
    return pl.pallas_call(
        _mod_kernel,
        grid=(n_out // blk,),
        in_specs=[pl.BlockSpec((rows, d), lambda j: (0, 0)),
                  pl.BlockSpec((d, blk), lambda j: (0, j)),
                  pl.BlockSpec((1, blk), lambda j: (0, j))],
        out_specs=pl.BlockSpec((rows, blk), lambda j: (0, j)),
        out_shape=jax.ShapeDtypeStruct((rows, n_out), F32),
        compiler_params=_params(("arbitrary",)),
        name="modulation",
    )(cvecs, w_mod, b_mod.reshape(1, n_out))


_PROJ_COLS = 256


def _proj_kernel(n_ctx_tiles, x_ref, ctx_ref, sh_ref, sc_ref, shc_ref, scc_ref, g_ref, w_ref,
                 kv_ref, cv_ref, qg_ref):
    is_ctx = pl.program_id(1) < n_ctx_tiles
    xin = jnp.where(is_ctx, ctx_ref[0], x_ref[0])
    sh = jnp.where(is_ctx, shc_ref[...], sh_ref[0])
    sc = jnp.where(is_ctx, scc_ref[...], sc_ref[0])
    ms = jnp.mean(xin * xin, axis=-1, keepdims=True)
    xn = (xin * lax.rsqrt(ms + RMS_EPS) * g_ref[...]) * (1.0 + sc) + sh
    xb = xn.astype(BF16)

    def emit(out_ref, col0, width, scale_cols=0):
        for j in range(0, width, _PROJ_COLS):
            res = _dot(xb, w_ref[:, col0 + j:col0 + j + _PROJ_COLS])
            if j < scale_cols:
                res = res * (HEAD_DIM ** -0.5 * LOG2E)
            out_ref[0, :, j:j + _PROJ_COLS] = res.astype(out_ref.dtype)

    emit(kv_ref, O_NA_K, 2 * WIDTH)
    emit(cv_ref, O_RW_K, CONV_W)
    emit(qg_ref, O_NA_Q, 3 * WIDTH, scale_cols=WIDTH)


def _input_projection(x, ctx, shift, scale, shift_c, scale_c, norm_g, w_in_bf16):
    b, t, d = x.shape
    l = ctx.shape[1]
    n_ctx_tiles = l // TILE
    n_tiles = n_ctx_tiles + t // TILE
    t_tot = l + t
    lat = lambda i, j: (i, jnp.maximum(j - n_ctx_tiles, 0), 0)
    cidx = lambda i, j: (i, jnp.minimum(j, n_ctx_tiles - 1), 0)
    per_b = lambda i, j: (i, 0, 0)
    fixed2 = lambda i, j: (0, 0)
    out_idx = lambda i, j: (i, j, 0)
    return pl.pallas_call(
        functools.partial(_proj_kernel, n_ctx_tiles),
        grid=(b, n_tiles),
        in_specs=[pl.BlockSpec((1, TILE, d), lat),
                  pl.BlockSpec((1, TILE, d), cidx),
                  pl.BlockSpec((1, 1, d), per_b),
                  pl.BlockSpec((1, 1, d), per_b),
                  pl.BlockSpec((1, d), fixed2),
                  pl.BlockSpec((1, d), fixed2),
                  pl.BlockSpec((1, d), fixed2),
                  pl.BlockSpec((d, D_IN), fixed2)],
        out_specs=[pl.BlockSpec((1, TILE, 2 * WIDTH), out_idx),
                   pl.BlockSpec((1, TILE, CONV_W), out_idx),
                   pl.BlockSpec((1, TILE, 3 * WIDTH), lat)],
        out_shape=[jax.ShapeDtypeStruct((b, t_tot, 2 * WIDTH), BF16),
                   jax.ShapeDtypeStruct((b, t_tot, CONV_W), BF16),
                   jax.ShapeDtypeStruct((b, t, 3 * WIDTH), BF16)],
        compiler_params=_params(("parallel", "arbitrary")),
        name="input_projection",
    )(x, ctx, shift, scale, shift_c, scale_c, norm_g, w_in_bf16)


def _prep_kernel(n_ctx_tiles, cv_ref, prev_ref, next_ref, cw_ref, w0_ref, w2_ref, a0_ref, a2_ref,
                 kk_w_ref, ka_ref, rk_ref, seg_ref, tri_ref, shift_ref,
                 at0, kt0, bt0, rt0, ec0, at1, kt1, bt1, rt1, ec1, v_ref, bonus_ref):
    tile = pl.program_id(1)
    n_tiles = pl.num_programs(1)
    tm = cv_ref.shape[1]
    first = jnp.logical_or(tile == 0, tile == n_ctx_tiles)
    last = jnp.logical_or(tile == n_ctx_tiles - 1, tile == n_tiles - 1)
    row8 = lax.broadcasted_iota(jnp.int32, (8, 1), 0)

    def conv(lo_col, hi_col):
        cols = slice(lo_col, hi_col)
        pb = cv_ref[0, :, cols]
        w_prev, w_mid, w_next = cw_ref[0:1, cols], cw_ref[1:2, cols], cw_ref[2:3, cols]
        u = _dot(shift_ref[0], pb) * w_prev + pb.astype(F32) * w_mid + _dot(shift_ref[1], pb) * w_next
        prev_row = jnp.where(first, 0.0, prev_ref[0, BF16_ROWS - 1:BF16_ROWS, cols].astype(F32))
        next_row = jnp.where(last, 0.0, next_ref[0, 0:1, cols].astype(F32))
        top = u[0:8] + jnp.where(row8 == 0, prev_row * w_prev, 0.0)
        bot = u[tm - 8:] + jnp.where(row8 == 7, next_row * w_next, 0.0)
        return jnp.concatenate([top, u[8:tm - 8], bot], axis=0)

    k = conv(0, WIDTH)
    v = conv(WIDTH, 2 * WIDTH)
    lora_in = conv(2 * WIDTH, 2 * WIDTH + 4 * LORA)
    wd = jnp.tanh(lora_in[:, :2 * LORA]).astype(BF16)
    ad = lora_in[:, 2 * LORA:].astype(BF16)
    r = conv(2 * WIDTH + 4 * LORA, CONV_W)
    seg = seg_ref[...]

    kk = k * kk_w_ref[...]
    kk = kk * lax.rsqrt(jnp.maximum(_split_dot_right(kk * kk, seg, 2), 1e-24))
    v_ref[0] = v.astype(BF16)
    k_ka = k * ka_ref[...]
    k_rest = k - k_ka
    r_rk = r * rk_ref[...]
    lw_scale = np.float32(-np.exp(-0.5) * LOG2E)

    kd_sum = None
    outs = ((at0, kt0, bt0, rt0, ec0), (at1, kt1, bt1, rt1, ec1))
    for d, (at, kt, bt, rt, ec) in enumerate(outs):
        lw = lw_scale * _sigmoid(w0_ref[d:d + 1, :] + _dot(wd, w2_ref[d]))
        a = _sigmoid(a0_ref[d:d + 1, :] + _dot(ad, a2_ref[d]))
        kd = k_rest + k_ka * a
        bb = kk * a
        kd_sum = kd if kd_sum is None else kd_sum + kd
        cs = _split_dot(tri_ref[d], lw, 2)
        e_inv = jnp.exp2(-cs)
        at[0] = (kk * jnp.exp2(cs - lw)).astype(BF16)
        kt[0] = (kd * e_inv).astype(BF16)
        bt[0] = (bb * e_inv).astype(BF16)
        rt[0] = (r * jnp.exp2(cs)).astype(BF16)
        for c in range(tm // CHUNK):
            end = c * CHUNK + (CHUNK - 1 if d == 0 else 0)
            ec[0, c] = jnp.exp2(cs[end:end + 1, :])
    bonus_ref[0] = _split_dot_right(r_rk * kd_sum, seg, 2) * v


def _rwkv_prep(cv, n_ctx_tiles, conv_w, w0, w2cat, a0, a2cat, k_k, k_a, r_k, seg, tri, shift):
    b, t_tot, w = cv.shape
    n_tiles = t_tot // TILE
    halo = TILE // BF16_ROWS
    main = lambda i, j: (i, j, 0)
    prev = lambda i, j: (i, jnp.maximum(j * halo - 1, 0), 0)
    nxt = lambda i, j: (i, jnp.minimum((j + 1) * halo, t_tot // BF16_ROWS - 1), 0)
    fixed2 = lambda i, j: (0, 0)
    fixed3 = lambda i, j: (0, 0, 0)
    tok = pl.BlockSpec((1, TILE, WIDTH), main)
    ecs = pl.BlockSpec((1, TILE // CHUNK, 1, WIDTH), lambda i, j: (i, j, 0, 0))
    tok_shape = jax.ShapeDtypeStruct((b, t_tot, WIDTH), BF16)
    ec_shape = jax.ShapeDtypeStruct((b, t_tot // CHUNK, 1, WIDTH), F32)
    per_dir_specs = [tok] * 4 + [ecs]
    per_dir_shapes = [tok_shape] * 4 + [ec_shape]
    return pl.pallas_call(
        functools.partial(_prep_kernel, n_ctx_tiles),
        grid=(b, n_tiles),
        in_specs=[pl.BlockSpec((1, TILE, w), main),
                  pl.BlockSpec((1, BF16_ROWS, w), prev),
                  pl.BlockSpec((1, BF16_ROWS, w), nxt),
                  pl.BlockSpec((3, w), fixed2),
                  pl.BlockSpec((2, WIDTH), fixed2),
                  pl.BlockSpec((2, 2 * LORA, WIDTH), fixed3),
                  pl.BlockSpec((2, WIDTH), fixed2),
                  pl.BlockSpec((2, 2 * LORA, WIDTH), fixed3),
                  pl.BlockSpec((1, WIDTH), fixed2),
                  pl.BlockSpec((1, WIDTH), fixed2),
                  pl.BlockSpec((1, WIDTH), fixed2),
                  pl.BlockSpec((WIDTH, WIDTH), fixed2),
                  pl.BlockSpec((2, TILE, TILE), fixed3),
                  pl.BlockSpec((2, TILE, TILE), fixed3)],
        out_specs=per_dir_specs * 2 + [tok, tok],
        out_shape=per_dir_shapes * 2 + [tok_shape, jax.ShapeDtypeStruct((b, t_tot, WIDTH), F32)],
        compiler_params=_params(("parallel", "arbitrary")),
        name="rwkv_prep",
    )(cv, cv, cv, conv_w, w0, w2cat, a0, a2cat, k_k, k_a, r_k, seg, tri, shift)


_INV_SQUARINGS = 5


def _stack(xp, lo):
    zero = jnp.zeros_like(xp)
    return jnp.concatenate([jnp.where(lo, xp, zero), jnp.where(lo, zero, xp)], axis=0)


def _chunk_operators(units, masks):
    lo, strict_f, incl_f, strict_r, incl_r, eye_w, eye_sq, same_head = masks
    cat = jnp.concatenate
    c = CHUNK
    b16 = lambda xs: [x.astype(BF16) for x in xs]
    stack = lambda xs: [_stack(x, lo) for x in xs]
    strict = [strict_r if u["rev"] else strict_f for u in units]
    incl = [incl_r if u["rev"] else incl_f for u in units]
    a_t, k_t, b_t, r_t, v = ([u[n] for u in units] for n in ("a_t", "k_t", "b_t", "r_t", "v"))
    ast, kst, bst, vst = stack(a_t), stack(k_t), stack(b_t), stack(v)
    scores = [_dot_nt(cat([a, r], axis=0), cat([ks, bs], axis=0)) for a, r, ks, bs in zip(a_t, r_t, kst, bst)]
    ak = b16([jnp.where(mk, sc[:c, :LANES], 0.0) for mk, sc in zip(strict, scores)])
    nm = [jnp.where(mk, -sc[:c, LANES:], 0.0) for mk, sc in zip(strict, scores)]
    rkb = b16([jnp.where(mk, sc[c:, :], 0.0) for mk, sc in zip(incl, scores)])
    w1st = stack(b16([_dot(x, vs) for x, vs in zip(ak, vst)]))
    nmb = b16(nm)
    pw = [_dot(x, xs) for x, xs in zip(nmb, stack(nmb))]
    tinv = [eye_w + x for x in nm]
    for _ in range(_INV_SQUARINGS - 1):
        pwb = b16(pw)
        prod = [_dot(cat([pb, t.astype(BF16)], axis=0), ps) for pb, t, ps in zip(pwb, tinv, stack(pwb))]
        pw = [x[:c] for x in prod]
        tinv = [t + x[c:] for t, x in zip(tinv, prod)]
    tinv = [t + _dot(t.astype(BF16), ps) for t, ps in zip(tinv, stack(b16(pw)))]
    gu = b16([_dot(t.astype(BF16), cat([x, w], axis=1)) for t, x, w in zip(tinv, ast, w1st)])
    g = [x[:, :LANES] for x in gu]
    u0 = [x[:, LANES:] for x in gu]
    q = b16([r.astype(F32) - _dot(rb[:, LANES:], gs) for r, rb, gs in zip(r_t, rkb, stack(g))])
    y0 = [_dot(rb, cat([vs, -us], axis=0)) for rb, vs, us in zip(rkb, vst, stack(u0))]
    k_h = b16([x.astype(F32) * u["e_c"] for x, u in zip(k_t, units)])
    b_h = b16([x.astype(F32) * u["e_c"] for x, u in zip(b_t, units)])
    m = b16([jnp.where(same_head, jnp.where(eye_sq, u["e_c"], 0.0) - _dot_tn(bh, gg), 0.0)
             for u, bh, gg in zip(units, b_h, g)])
    dd = [jnp.where(same_head, _dot_tn(cat([kh, bh], axis=0), cat([vv, -uu], axis=0)), 0.0)
          for kh, bh, vv, uu in zip(k_h, b_h, v, u0)]
    return [(cat([qq, mm], axis=0), yy, d2) for qq, mm, yy, d2 in zip(q, m, y0, dd)]


SCAN_CHUNKS = 4


def _scan_kernel(at0, kt0, bt0, rt0, ec0, v0, at1, kt1, bt1, rt1, ec1, v1, y0_ref, y1_ref, s_ref):
    @pl.when(pl.program_id(1) == 0)
    def _():
        s_ref[...] = jnp.zeros_like(s_ref)

    c = CHUNK
    lane_w = lax.broadcasted_iota(jnp.int32, (c, LANES), 1)
    t_w = lax.broadcasted_iota(jnp.int32, (c, LANES), 0)
    s_w = lane_w & (c - 1)
    t_w2 = lax.broadcasted_iota(jnp.int32, (c, 2 * LANES), 0)
    s_w2 = lax.broadcasted_iota(jnp.int32, (c, 2 * LANES), 1) & (c - 1)
    ri = lax.broadcasted_iota(jnp.int32, (LANES, LANES), 0)
    ci = lax.broadcasted_iota(jnp.int32, (LANES, LANES), 1)
    masks = (lane_w < HEAD_DIM,
             t_w > s_w, t_w2 >= s_w2, t_w < s_w, t_w2 <= s_w2,
             jnp.where(t_w == s_w, 1.0, 0.0), ri == ci, (ri // HEAD_DIM) == (ci // HEAD_DIM))
    dirs = ((at0, kt0, bt0, rt0, ec0, v0, y0_ref), (at1, kt1, bt1, rt1, ec1, v1, y1_ref))
    units = []
    where = []
    for j in range(SCAN_CHUNKS):
        for d, (at, kt, bt, rt, ec, vv, y_ref) in enumerate(dirs):
            pos = j if d == 0 else SCAN_CHUNKS - 1 - j
            rows = slice(pos * c, (pos + 1) * c)
            for pr in range(N_PAIRS):
                sl = slice(pr * LANES, (pr + 1) * LANES)
                units.append(dict(rev=d == 1, a_t=at[0, rows, sl], k_t=kt[0, rows, sl], b_t=bt[0, rows, sl],
                                  r_t=rt[0, rows, sl], v=vv[0, rows, sl], e_c=ec[0, pos, :, sl]))
                where.append((d, pr, y_ref, rows, sl))
    ops = _chunk_operators(units, masks)
    n_seq = 2 * N_PAIRS
    state = [s_ref[d, pr] for (d, pr, _, _, _) in where[:n_seq]]
    for j in range(SCAN_CHUNKS):
        step_ops = ops[j * n_seq:(j + 1) * n_seq]
        prods = [_dot(qm, s.astype(BF16)) for (qm, _, _), s in zip(step_ops, state)]
        for x, (_, y0, _), (_, _, y_ref, rows, sl) in zip(prods, step_ops, where[j * n_seq:(j + 1) * n_seq]):
            y_ref[0, rows, sl] = x[:c] + y0
        state = [x[c:] + dd for x, (_, _, dd) in zip(prods, step_ops)]
    for s, (d, pr, _, _, _) in zip(state, where[:n_seq]):
        s_ref[d, pr] = s


def _rwkv_scan(prep, n_ctx_chunks):
    (at0, kt0, bt0, rt0, ec0, at1, kt1, bt1, rt1, ec1, v, _) = prep
    b, t_tot, _ = v.shape
    assert t_tot % (SCAN_CHUNKS * CHUNK) == 0 and n_ctx_chunks % SCAN_CHUNKS == 0
    n_blocks = t_tot // (SCAN_CHUNKS * CHUNK)
    n_ctx_blocks = n_ctx_chunks // SCAN_CHUNKS

    def rev_block(g):
        return jnp.where(g < n_ctx_blocks, n_ctx_blocks - 1 - g, n_blocks - 1 + n_ctx_blocks - g)

    fwd = lambda i, g: (i, g, 0)
    rev = lambda i, g: (i, rev_block(g), 0)
    fwd4 = lambda i, g: (i, g, 0, 0)
    rev4 = lambda i, g: (i, rev_block(g), 0, 0)
    tokf = pl.BlockSpec((1, SCAN_CHUNKS * CHUNK, WIDTH), fwd)
    tokr = pl.BlockSpec((1, SCAN_CHUNKS * CHUNK, WIDTH), rev)
    ecf = pl.BlockSpec((1, SCAN_CHUNKS, 1, WIDTH), fwd4)
    ecr = pl.BlockSpec((1, SCAN_CHUNKS, 1, WIDTH), rev4)
    y_shape = jax.ShapeDtypeStruct((b, t_tot, WIDTH), F32)
    return pl.pallas_call(
        _scan_kernel,
        grid=(b, n_blocks),
        in_specs=[tokf] * 4 + [ecf, tokf] + [tokr] * 4 + [ecr, tokr],
        out_specs=[tokf, tokr],
        out_shape=[y_shape, y_shape],
        scratch_shapes=[pltpu.VMEM((2, N_PAIRS, 2 * HEAD_DIM, 2 * HEAD_DIM), F32)],
        compiler_params=_params(("parallel", "arbitrary")),
        name="rwkv_scan",
    )(at0, kt0, bt0, rt0, ec0, v, at1, kt1, bt1, rt1, ec1, v)


NA_ROWS = 16


def _na_kernel(ctx_len, n_rows, q_ref, k_ref, v_ref, bias_ref, o_ref):
    step = pl.program_id(2)
    win = NA_KH * GRID_W
    kc = k_ref[0, 0:ctx_len, :]
    vc = v_ref[0, 0:ctx_len, :]
    lo = lax.broadcasted_iota(jnp.int32, (GRID_W, LANES), 1) < HEAD_DIM
    rows = range(NA_ROWS)
    r0 = [jnp.clip(step * NA_ROWS + j - NA_KH // 2, 0, n_rows - NA_KH) for j in rows]
    delta = [step * NA_ROWS + j - r for j, r in zip(rows, r0)]
    start = [pl.multiple_of(ctx_len + r * GRID_W, GRID_W) for r in r0]
    qst = [_stack(q_ref[0, j * GRID_W:(j + 1) * GRID_W, :], lo) for j in rows]
    s_loc = [_dot_nt(q, k_ref[0, pl.ds(s, win), :]) + bias_ref[dl, 0] for q, s, dl in zip(qst, start, delta)]
    s_ctx = [_dot_nt(q, kc) for q in qst]
    mx = [jnp.maximum(jnp.max(a, axis=-1, keepdims=True), jnp.max(c, axis=-1, keepdims=True))
          for a, c in zip(s_loc, s_ctx)]
    p_loc = [jnp.exp2(a - m).astype(BF16) for a, m in zip(s_loc, mx)]
    p_ctx = [jnp.exp2(c - m).astype(BF16) for c, m in zip(s_ctx, mx)]
    ones_loc = jnp.ones((win, LANES), BF16)
    ones_ctx = jnp.ones((ctx_len, LANES), BF16)
    vc1 = jnp.concatenate([vc, ones_ctx], axis=1)
    od = [_dot(a, jnp.concatenate([v_ref[0, pl.ds(s, win), :], ones_loc], axis=1)) + _dot(c, vc1)
          for a, c, s in zip(p_loc, p_ctx, start)]
    for j, x in zip(rows, od):
        oj = x[:, :LANES] / x[:, LANES:]
        o_ref[0, j * GRID_W:(j + 1) * GRID_W, :] = jnp.where(lo, oj[:GRID_W], oj[GRID_W:]).astype(o_ref.dtype)


def _neighborhood_attention(qg, kv, bias, ctx_len):
    b, t_tot, _ = kv.shape
    t = t_tot - ctx_len
    n_rows = t // GRID_W
    assert n_rows % NA_ROWS == 0
    blk_rows = NA_ROWS * GRID_W
    return pl.pallas_call(
        functools.partial(_na_kernel, ctx_len, n_rows),
        grid=(b, N_PAIRS, n_rows // NA_ROWS),
        in_specs=[pl.BlockSpec((1, blk_rows, LANES), lambda bi, pr, i: (bi, i, pr)),
                  pl.BlockSpec((1, t_tot, LANES), lambda bi, pr, i: (bi, 0, pr)),
                  pl.BlockSpec((1, t_tot, LANES), lambda bi, pr, i: (bi, 0, N_PAIRS + pr)),
                  pl.BlockSpec((NA_KH, 1, 2 * GRID_W, NA_KH * GRID_W), lambda bi, pr, i: (0, pr, 0, 0))],
        out_specs=pl.BlockSpec((1, blk_rows, LANES), lambda bi, pr, i: (bi, i, pr)),
        out_shape=jax.ShapeDtypeStruct((b, t, WIDTH), BF16),
        compiler_params=_params(("parallel", "parallel", "arbitrary")),
        name="neighborhood_attention",
    )(qg, kv, kv, bias)


def _na_bias_table(na_rpb):
    col = np.arange(GRID_W)
    c0 = np.clip(col - NA_KW // 2, 0, GRID_W - NA_KW)
    valid = (col[None, :] >= c0[:, None]) & (col[None, :] < c0[:, None] + NA_KW)
    col_off = np.clip(col[None, :] - col[:, None] + NA_KW - 1, 0, 2 * NA_KW - 2)
    row_off = np.arange(NA_KH)[None, :] - np.arange(NA_KH)[:, None] + NA_KH - 1
    sel_r = (row_off[..., None] == np.arange(2 * NA_KH - 1)).astype(np.float32)
    sel_c = ((col_off[..., None] == np.arange(2 * NA_KW - 1)) & valid[..., None]).astype(np.float32)
    tbl = jnp.einsum('dri,hij,qcj->dhqrc', sel_r, na_rpb.astype(F32), sel_c, precision=lax.Precision.HIGHEST)
    tbl = tbl * LOG2E + jnp.asarray(np.where(valid, 0.0, NEG_BIG).astype(np.float32))[None, None, :, None, :]
    return tbl.reshape(NA_KH, N_PAIRS, 2 * GRID_W, NA_KH * GRID_W)


def _out_kernel(yf_ref, yr_ref, bonus_ref, na_ref, nag_ref, rwg_ref, x_ref, gate_ref, wo_ref,
                gnw_ref, gnb_ref, fg_ref, seg_ref, o_ref):
    seg = seg_ref[...]
    y = yf_ref[0] + yr_ref[0]
    inv_n = 1.0 / HEAD_DIM
    mu = _split_dot_right(y, seg, 2) * inv_n
    dev = y - mu
    var = _split_dot_right(dev * dev, seg, 2) * inv_n
    yn = dev * lax.rsqrt(var + GN_EPS) * gnw_ref[...] + gnb_ref[...]
    rwg = rwg_ref[0].astype(F32)
    nag = nag_ref[0].astype(F32)
    rw = (yn + bonus_ref[0]) * (rwg * _sigmoid(rwg))
    na = na_ref[0].astype(F32) * (nag * _sigmoid(nag))
    mix = jnp.concatenate([na, rw], axis=-1).astype(BF16)
    h = x_ref[0] + gate_ref[0] * _dot(mix, wo_ref[...])
    ms = jnp.mean(h * h, axis=-1, keepdims=True)
    o_ref[0] = h * lax.rsqrt(ms + RMS_EPS) * fg_ref[...]


def _readout_project(yf, yr, bonus, na, qg, x, gate, w_out_bf16, gn_w, gn_b, final_g, seg, n_ctx_tiles):
    b, t, d = x.shape
    lat = lambda i, j: (i, j, 0)
    off = lambda i, j: (i, j + n_ctx_tiles, 0)
    fixed2 = lambda i, j: (0, 0)
    tok_off = pl.BlockSpec((1, TILE, WIDTH), off)
    return pl.pallas_call(
        _out_kernel,
        grid=(b, t // TILE),
        in_specs=[tok_off, tok_off, tok_off,
                  pl.BlockSpec((1, TILE, WIDTH), lat),
                  pl.BlockSpec((1, TILE, WIDTH), lambda i, j: (i, j, 1)),
                  pl.BlockSpec((1, TILE, WIDTH), lambda i, j: (i, j, 2)),
                  pl.BlockSpec((1, TILE, d), lat),
                  pl.BlockSpec((1, 1, d), lambda i, j: (i, 0, 0)),
                  pl.BlockSpec((2 * WIDTH, d), fixed2),
                  pl.BlockSpec((1, WIDTH), fixed2),
                  pl.BlockSpec((1, WIDTH), fixed2),
                  pl.BlockSpec((1, d), fixed2),
                  pl.BlockSpec((WIDTH, WIDTH), fixed2)],
        out_specs=pl.BlockSpec((1, TILE, d), lat),
        out_shape=jax.ShapeDtypeStruct((b, t, d), F32),
        compiler_params=_params(("parallel", "arbitrary")),
        name="readout_project",
    )(yf, yr, bonus, na, qg, qg, x, gate, w_out_bf16, gn_w, gn_b, final_g, seg)


def _chunk_matrices():
    idx = np.arange(TILE)
    same = (idx[:, None] // CHUNK) == (idx[None, :] // CHUNK)
    tri_f = same & (idx[:, None] >= idx[None, :])
    tri_r = same & (idx[:, None] <= idx[None, :])
    tri = jnp.asarray(np.stack([tri_f, tri_r]).astype(np.float32), dtype=BF16)
    hid = np.arange(WIDTH) // HEAD_DIM
    seg = jnp.asarray((hid[:, None] == hid[None, :]).astype(np.float32), dtype=BF16)
    down = idx[:, None] == idx[None, :] + 1
    shift = jnp.asarray(np.stack([down, down.T]).astype(np.float32), dtype=BF16)
    return tri, seg, shift


def _lora_cat(w2):
    z = jnp.zeros_like(w2[0])
    return jnp.stack([jnp.concatenate([w2[0], z], axis=0), jnp.concatenate([z, w2[1]], axis=0)]).astype(BF16)


def kernel(x, c, ctx, c_ctx, w_mod, b_mod, norm_g, w_in, conv_w, decay_w0, decay_w2, aaa_a0, aaa_a2,
           k_k, k_a, r_k, gn_w, gn_b, na_rpb, w_out, final_g):
    depth = w_mod.shape[0]
    assert depth == 1, "single-layer block: the context stream is never updated"
    b, t, d = x.shape
    l = ctx.shape[1]
    assert l == TILE and t % TILE == 0 and t // GRID_W >= NA_KH and w_in.shape[2] == D_IN
    n_ctx_tiles = l // TILE

    rows = -(-(b + 1) // 8) * 8
    cvecs = jnp.zeros((rows, d), F32).at[:b].set(c).at[b].set(c_ctx)
    mod = _modulation(cvecs, w_mod[0], b_mod[0])
    shift, scale, gate = mod[:, :d], mod[:, d:2 * d], mod[:, 2 * d:]

    kv, cv, qg = _input_projection(
        x, ctx, shift[:b].reshape(b, 1, d), scale[:b].reshape(b, 1, d), shift[b:b + 1], scale[b:b + 1],
        norm_g[0].reshape(1, d), w_in[0].astype(BF16))

    tri, seg, shift = _chunk_matrices()
    prep = _rwkv_prep(cv, n_ctx_tiles, conv_w[0], decay_w0[0], _lora_cat(decay_w2[0]), aaa_a0[0],
                      _lora_cat(aaa_a2[0]), k_k[0].reshape(1, WIDTH), k_a[0].reshape(1, WIDTH),
                      r_k[0].reshape(1, WIDTH), seg, tri, shift)
    yf, yr = _rwkv_scan(prep, l // CHUNK)
    na = _neighborhood_attention(qg, kv, _na_bias_table(na_rpb[0]), l)
    return _readout_project(yf, yr, prep[-1], na, qg, x, gate[:b].reshape(b, 1, d), w_out[0].astype(BF16),
                            gn_w[0].reshape(1, WIDTH), gn_b[0].reshape(1, WIDTH), final_g.reshape(1, d), seg,
                            n_ctx_tiles)
```

```python
import functools

import numpy as np
import jax
import jax.numpy as jnp
from jax import lax
from jax.experimental import pallas as pl
from jax.experimental.pallas import tpu as pltpu

F32 = jnp.float32
BF16 = jnp.bfloat16

HEAD_DIM = 64
LANES = 128
NA_HEADS = 8
RW_HEADS = 8
WIDTH = NA_HEADS * HEAD_DIM
N_PAIRS = WIDTH // LANES
GRID_W = 64
NA_KH = 8
NA_KW = 16
LORA = 64
CHUNK = 64
TILE = 256
RMS_EPS = 1e-6
GN_EPS = 64e-5
NEG_BIG = -1e30
LOG2E = 1.4426950408889634
VMEM_LIMIT = 56 * 1024 * 1024

O_NA_K = 0
O_NA_V = WIDTH
O_RW_K = 2 * WIDTH
CONV_W = 3 * WIDTH + 4 * LORA
O_NA_Q = O_RW_K + CONV_W
D_IN = O_NA_Q + 3 * WIDTH


def _dot(a, b):
    return jnp.dot(a, b, preferred_element_type=F32)


def _dot_nt(a, b):
    return lax.dot_general(a, b, (((1,), (1,)), ((), ())), preferred_element_type=F32)


def _dot_tn(a, b):
    return lax.dot_general(a, b, (((0,), (0,)), ((), ())), preferred_element_type=F32)


def _split_dot(m, x, terms):
    acc = None
    rem = x
    for _ in range(terms):
        piece = rem.astype(BF16)
        part = _dot(m, piece)
        acc = part if acc is None else acc + part
        rem = rem - piece.astype(F32)
    return acc


def _split_dot_right(x, m, terms):
    acc = None
    rem = x
    for _ in range(terms):
        piece = rem.astype(BF16)
        part = _dot(piece, m)
        acc = part if acc is None else acc + part
        rem = rem - piece.astype(F32)
    return acc


def _sigmoid(z):
    return 1.0 / (1.0 + jnp.exp2(z * (-LOG2E)))


def _params(sem):
    return pltpu.CompilerParams(dimension_semantics=sem, vmem_limit_bytes=VMEM_LIMIT)


def _mod_kernel(c_ref, w_ref, b_ref, o_ref):
    cv = c_ref[...]
    o_ref[...] = _dot(cv * _sigmoid(cv), w_ref[...]) + b_ref[...]


def _modulation(cvecs, w_mod, b_mod):
    rows, d = cvecs.shape
    n_out = w_mod.shape[1]
    blk = d
    return pl.pallas_call(
        _mod_kernel,
        grid=(n_out // blk,),
        in_specs=[pl.BlockSpec((rows, d), lambda j: (0, 0)),
                  pl.BlockSpec((d, blk), lambda j: (0, j)),
                  pl.BlockSpec((1, blk), lambda j: (0, j))],
        out_specs=pl.BlockSpec((rows, blk), lambda j: (0, j)),
        out_shape=jax.ShapeDtypeStruct((rows, n_out), F32),
        compiler_params=_params(("arbitrary",)),
        name="modulation",
    )(cvecs, w_mod, b_mod.reshape(1, n_out))


_PROJ_COLS = 256


def _proj_prep_kernel(n_ctx_tiles, x_ref, xprev_ref, xnext_ref, ctx_ref, sh_ref, sc_ref, shc_ref, scc_ref, g_ref,
                      w_ref, cw_ref, w0_ref, w2_ref, a0_ref, a2_ref, kk_w_ref, ka_ref, rk_ref, seg_ref, tri_ref,
                      kv_ref, qg_ref, at0, kt0, bt0, rt0, ec0, at1, kt1, bt1, rt1, ec1, v_ref,
                      bonus_ref, cv_scr, halo_scr):
    tile = pl.program_id(1)
    n_tiles = pl.num_programs(1)
    is_ctx = tile < n_ctx_tiles
    gain = g_ref[...]

    def normed(xin, sh, sc):
        ms = jnp.mean(xin * xin, axis=-1, keepdims=True)
        return ((xin * lax.rsqrt(ms + RMS_EPS) * gain) * (1.0 + sc) + sh).astype(BF16)

    xb = normed(jnp.where(is_ctx, ctx_ref[0], x_ref[0]), jnp.where(is_ctx, shc_ref[...], sh_ref[0]),
                jnp.where(is_ctx, scc_ref[...], sc_ref[0]))

    def project(store, col0, j, scaled):
        res = _dot(xb, w_ref[:, col0 + j:col0 + j + _PROJ_COLS])
        if scaled:
            res = res * (HEAD_DIM ** -0.5 * LOG2E)
        store(slice(j, j + _PROJ_COLS), res.astype(BF16))

    def chunks(store, col0, width, scale_cols=0):
        return [functools.partial(project, store, col0, j, j < scale_cols) for j in range(0, width, _PROJ_COLS)]

    def to_kv(cols, val):
        kv_ref[0, :, cols] = val

    def to_cv(cols, val):
        cv_scr[:, cols] = val

    def to_qg(cols, val):
        qg_ref[0, :, cols] = val

    conv_chunks = chunks(to_cv, O_RW_K, CONV_W)
    pending = chunks(to_kv, O_NA_K, 2 * WIDTH) + chunks(to_qg, O_NA_Q, 3 * WIDTH, scale_cols=WIDTH)

    def more(queue=pending, n=1):
        for _ in range(n):
            if queue:
                queue.pop(0)()

    hb = normed(jnp.concatenate([xprev_ref[0], xnext_ref[0]], axis=0), sh_ref[0], sc_ref[0])
    halo_scr[...] = _dot(hb, w_ref[:, O_RW_K:O_RW_K + CONV_W]).astype(BF16).astype(F32)

    tm = cv_scr.shape[0]
    first = jnp.logical_or(tile == 0, tile == n_ctx_tiles)
    last = jnp.logical_or(tile == n_ctx_tiles - 1, tile == n_tiles - 1)
    row8 = lax.broadcasted_iota(jnp.int32, (8, 1), 0)

    def conv(lo_col, hi_col):
        cols = slice(lo_col, hi_col)
        p = cv_scr[:, cols].astype(F32)
        w_prev, w_mid, w_next = cw_ref[0:1, cols], cw_ref[1:2, cols], cw_ref[2:3, cols]
        u = pltpu.roll(p, 1, axis=0) * w_prev + p * w_mid + pltpu.roll(p, tm - 1, axis=0) * w_next
        prev_row = jnp.where(first, 0.0, halo_scr[7:8, cols])
        next_row = jnp.where(last, 0.0, halo_scr[8:9, cols])
        top = u[0:8] + jnp.where(row8 == 0, (prev_row - p[tm - 1:tm]) * w_prev, 0.0)
        bot = u[tm - 8:] + jnp.where(row8 == 7, (next_row - p[0:1]) * w_next, 0.0)
        return jnp.concatenate([top, u[8:tm - 8], bot], axis=0)

    per_group = WIDTH // _PROJ_COLS
    more(conv_chunks, per_group)
    k = conv(0, WIDTH)
    more(conv_chunks, per_group)
    v = conv(WIDTH, 2 * WIDTH)
    more(conv_chunks, 4 * LORA // _PROJ_COLS)
    lora_in = conv(2 * WIDTH, 2 * WIDTH + 4 * LORA)
    wd = jnp.tanh(lora_in[:, :2 * LORA]).astype(BF16)
    ad = lora_in[:, 2 * LORA:].astype(BF16)
    more(conv_chunks, per_group)
    assert not conv_chunks
    r = conv(2 * WIDTH + 4 * LORA, CONV_W)
    seg = seg_ref[...]

    kk = k * kk_w_ref[...]
    kk = kk * lax.rsqrt(jnp.maximum(_split_dot_right(kk * kk, seg, 2), 1e-24))
    more()
    v_ref[0] = v.astype(BF16)
    k_ka = k * ka_ref[...]
    k_rest = k - k_ka
    r_rk = r * rk_ref[...]
    lw_scale = np.float32(-np.exp(-0.5) * LOG2E)

    kd_sum = None
    outs = ((at0, kt0, bt0, rt0, ec0), (at1, kt1, bt1, rt1, ec1))
    for d, (at, kt, bt, rt, ec) in enumerate(outs):
        lw = lw_scale * _sigmoid(w0_ref[d:d + 1, :] + _dot(wd, w2_ref[d]))
        more()
        a = _sigmoid(a0_ref[d:d + 1, :] + _dot(ad, a2_ref[d]))
        kd = k_rest + k_ka * a
        bb = kk * a
        kd_sum = kd if kd_sum is None else kd_sum + kd
        more()
        cs = _split_dot(tri_ref[d], lw, 2)
        e_inv = jnp.exp2(-cs)
        at[0] = (kk * jnp.exp2(cs - lw)).astype(BF16)
        more()
        kt[0] = (kd * e_inv).astype(BF16)
        bt[0] = (bb * e_inv).astype(BF16)
        more()
        rt[0] = (r * jnp.exp2(cs)).astype(BF16)
        for c in range(tm // CHUNK):
            end = c * CHUNK + (CHUNK - 1 if d == 0 else 0)
            ec[0, c] = jnp.exp2(cs[end:end + 1, :])
    bonus_ref[0] = _split_dot_right(r_rk * kd_sum, seg, 2) * v
    while pending:
        more()


def _project_and_prepare(x, ctx, shift_m, scale_m, shift_c, scale_c, norm_g, w_in_bf16,
                         conv_w, w0, w2cat, a0, a2cat, k_k, k_a, r_k, seg, tri):
    b, t, d = x.shape
    l = ctx.shape[1]
    n_ctx_tiles = l // TILE
    n_tiles = n_ctx_tiles + t // TILE
    t_tot = l + t
    rows8 = TILE // 8
    lat = lambda i, j: (i, jnp.maximum(j - n_ctx_tiles, 0), 0)
    prev = lambda i, j: (i, jnp.maximum((j - n_ctx_tiles) * rows8 - 1, 0), 0)
    nxt = lambda i, j: (i, jnp.clip((j - n_ctx_tiles + 1) * rows8, 0, t // 8 - 1), 0)
    cidx = lambda i, j: (i, jnp.minimum(j, n_ctx_tiles - 1), 0)
    per_b = lambda i, j: (i, 0, 0)
    fixed2 = lambda i, j: (0, 0)
    fixed3 = lambda i, j: (0, 0, 0)
    out_idx = lambda i, j: (i, j, 0)
    tok = pl.BlockSpec((1, TILE, WIDTH), out_idx)
    ecs = pl.BlockSpec((1, TILE // CHUNK, 1, WIDTH), lambda i, j: (i, j, 0, 0))
    tok_shape = jax.ShapeDtypeStruct((b, t_tot, WIDTH), BF16)
    ec_shape = jax.ShapeDtypeStruct((b, t_tot // CHUNK, 1, WIDTH), F32)
    per_dir_specs = [tok] * 4 + [ecs]
    per_dir_shapes = [tok_shape] * 4 + [ec_shape]
    return pl.pallas_call(
        functools.partial(_proj_prep_kernel, n_ctx_tiles),
        grid=(b, n_tiles),
        in_specs=[pl.BlockSpec((1, TILE, d), lat),
                  pl.BlockSpec((1, 8, d), prev),
                  pl.BlockSpec((1, 8, d), nxt),
                  pl.BlockSpec((1, TILE, d), cidx),
                  pl.BlockSpec((1, 1, d), per_b),
                  pl.BlockSpec((1, 1, d), per_b),
                  pl.BlockSpec((1, d), fixed2),
                  pl.BlockSpec((1, d), fixed2),
                  pl.BlockSpec((1, d), fixed2),
                  pl.BlockSpec((d, D_IN), fixed2),
                  pl.BlockSpec((3, CONV_W), fixed2),
                  pl.BlockSpec((2, WIDTH), fixed2),
                  pl.BlockSpec((2, 2 * LORA, WIDTH), fixed3),
                  pl.BlockSpec((2, WIDTH), fixed2),
                  pl.BlockSpec((2, 2 * LORA, WIDTH), fixed3),
                  pl.BlockSpec((1, WIDTH), fixed2),
                  pl.BlockSpec((1, WIDTH), fixed2),
                  pl.BlockSpec((1, WIDTH), fixed2),
                  pl.BlockSpec((WIDTH, WIDTH), fixed2),
                  pl.BlockSpec((2, TILE, TILE), fixed3)],
        out_specs=[pl.BlockSpec((1, TILE, 2 * WIDTH), out_idx),
                   pl.BlockSpec((1, TILE, 3 * WIDTH), lat)] + per_dir_specs * 2 + [tok, tok],
        out_shape=[jax.ShapeDtypeStruct((b, t_tot, 2 * WIDTH), BF16),
                   jax.ShapeDtypeStruct((b, t, 3 * WIDTH), BF16)] + per_dir_shapes * 2
                  + [tok_shape, jax.ShapeDtypeStruct((b, t_tot, WIDTH), F32)],
        scratch_shapes=[pltpu.VMEM((TILE, CONV_W), BF16), pltpu.VMEM((16, CONV_W), F32)],
        compiler_params=_params(("parallel", "arbitrary")),
        name="project_and_prepare",
    )(x, x, x, ctx, shift_m, scale_m, shift_c, scale_c, norm_g, w_in_bf16,
      conv_w, w0, w2cat, a0, a2cat, k_k, k_a, r_k, seg, tri)


_INV_SQUARINGS = 5


def _stack(xp, lo):
    zero = jnp.zeros_like(xp)
    return jnp.concatenate([jnp.where(lo, xp, zero), jnp.where(lo, zero, xp)], axis=0)


def _chunk_operators(units, masks):
    lo, strict_f, incl_f, strict_r, incl_r, eye_w, eye_sq, same_head = masks
    cat = jnp.concatenate
    c = CHUNK
    b16 = lambda xs: [x.astype(BF16) for x in xs]
    stack = lambda xs: [_stack(x, lo) for x in xs]
    strict = [strict_r if u["rev"] else strict_f for u in units]
    incl = [incl_r if u["rev"] else incl_f for u in units]
    a_t, k_t, b_t, r_t, v = ([u[n] for u in units] for n in ("a_t", "k_t", "b_t", "r_t", "v"))
    ast, kst, bst, vst = stack(a_t), stack(k_t), stack(b_t), stack(v)
    scores = [_dot_nt(cat([a, r], axis=0), cat([ks, bs], axis=0)) for a, r, ks, bs in zip(a_t, r_t, kst, bst)]
    ak = b16([jnp.where(mk, sc[:c, :LANES], 0.0) for mk, sc in zip(strict, scores)])
    nm = [jnp.where(mk, -sc[:c, LANES:], 0.0) for mk, sc in zip(strict, scores)]
    rkb = b16([jnp.where(mk, sc[c:, :], 0.0) for mk, sc in zip(incl, scores)])
    w1st = stack(b16([_dot(x, vs) for x, vs in zip(ak, vst)]))
    nmb = b16(nm)
    pw = [_dot(x, xs) for x, xs in zip(nmb, stack(nmb))]
    tinv = [eye_w + x for x in nm]
    for _ in range(_INV_SQUARINGS - 1):
        pwb = b16(pw)
        prod = [_dot(cat([pb, t.astype(BF16)], axis=0), ps) for pb, t, ps in zip(pwb, tinv, stack(pwb))]
        pw = [x[:c] for x in prod]
        tinv = [t + x[c:] for t, x in zip(tinv, prod)]
    tinv = [t + _dot(t.astype(BF16), ps) for t, ps in zip(tinv, stack(b16(pw)))]
    gu = b16([_dot(t.astype(BF16), cat([x, w], axis=1)) for t, x, w in zip(tinv, ast, w1st)])
    g = [x[:, :LANES] for x in gu]
    u0 = [x[:, LANES:] for x in gu]
    q = b16([r.astype(F32) - _dot(rb[:, LANES:], gs) for r, rb, gs in zip(r_t, rkb, stack(g))])
    y0 = [_dot(rb, cat([vs, -us], axis=0)) for rb, vs, us in zip(rkb, vst, stack(u0))]
    k_h = b16([x.astype(F32) * u["e_c"] for x, u in zip(k_t, units)])
    b_h = b16([x.astype(F32) * u["e_c"] for x, u in zip(b_t, units)])
    m = b16([jnp.where(same_head, jnp.where(eye_sq, u["e_c"], 0.0) - _dot_tn(bh, gg), 0.0)
             for u, bh, gg in zip(units, b_h, g)])
    dd = [jnp.where(same_head, _dot_tn(cat([kh, bh], axis=0), cat([vv, -uu], axis=0)), 0.0)
          for kh, bh, vv, uu in zip(k_h, b_h, v, u0)]
    return [(cat([qq, mm], axis=0), yy, d2) for qq, mm, yy, d2 in zip(q, m, y0, dd)]


SCAN_CHUNKS = 4


def _scan_kernel(at0, kt0, bt0, rt0, ec0, v0, at1, kt1, bt1, rt1, ec1, v1, y0_ref, y1_ref, s_ref):
    @pl.when(pl.program_id(1) == 0)
    def _():
        s_ref[...] = jnp.zeros_like(s_ref)

    c = CHUNK
    lane_w = lax.broadcasted_iota(jnp.int32, (c, LANES), 1)
    t_w = lax.broadcasted_iota(jnp.int32, (c, LANES), 0)
    s_w = lane_w & (c - 1)
    t_w2 = lax.broadcasted_iota(jnp.int32, (c, 2 * LANES), 0)
    s_w2 = lax.broadcasted_iota(jnp.int32, (c, 2 * LANES), 1) & (c - 1)
    ri = lax.broadcasted_iota(jnp.int32, (LANES, LANES), 0)
    ci = lax.broadcasted_iota(jnp.int32, (LANES, LANES), 1)
    masks = (lane_w < HEAD_DIM,
             t_w > s_w, t_w2 >= s_w2, t_w < s_w, t_w2 <= s_w2,
             jnp.where(t_w == s_w, 1.0, 0.0), ri == ci, (ri // HEAD_DIM) == (ci // HEAD_DIM))
    dirs = ((at0, kt0, bt0, rt0, ec0, v0, y0_ref), (at1, kt1, bt1, rt1, ec1, v1, y1_ref))
    units = []
    where = []
    for j in range(SCAN_CHUNKS):
        for d, (at, kt, bt, rt, ec, vv, y_ref) in enumerate(dirs):
            pos = j if d == 0 else SCAN_CHUNKS - 1 - j
            rows = slice(pos * c, (pos + 1) * c)
            for pr in range(N_PAIRS):
                sl = slice(pr * LANES, (pr + 1) * LANES)
                units.append(dict(rev=d == 1, a_t=at[0, rows, sl], k_t=kt[0, rows, sl], b_t=bt[0, rows, sl],
                                  r_t=rt[0, rows, sl], v=vv[0, rows, sl], e_c=ec[0, pos, :, sl]))
                where.append((d, pr, y_ref, rows, sl))
    ops = _chunk_operators(units, masks)
    n_seq = 2 * N_PAIRS
    state = [s_ref[d, pr] for (d, pr, _, _, _) in where[:n_seq]]
    for j in range(SCAN_CHUNKS):
        step_ops = ops[j * n_seq:(j + 1) * n_seq]
        prods = [_dot(qm, s.astype(BF16)) for (qm, _, _), s in zip(step_ops, state)]
        for x, (_, y0, _), (_, _, y_ref, rows, sl) in zip(prods, step_ops, where[j * n_seq:(j + 1) * n_seq]):
            y_ref[0, rows, sl] = x[:c] + y0
        state = [x[c:] + dd for x, (_, _, dd) in zip(prods, step_ops)]
    for s, (d, pr, _, _, _) in zip(state, where[:n_seq]):
        s_ref[d, pr] = s


def _rwkv_scan(prep, n_ctx_chunks):
    (at0, kt0, bt0, rt0, ec0, at1, kt1, bt1, rt1, ec1, v, _) = prep
    b, t_tot, _ = v.shape
    assert t_tot % (SCAN_CHUNKS * CHUNK) == 0 and n_ctx_chunks % SCAN_CHUNKS == 0
    n_blocks = t_tot // (SCAN_CHUNKS * CHUNK)
    n_ctx_blocks = n_ctx_chunks // SCAN_CHUNKS

    def rev_block(g):
        return jnp.where(g < n_ctx_blocks, n_ctx_blocks - 1 - g, n_blocks - 1 + n_ctx_blocks - g)

    fwd = lambda i, g: (i, g, 0)
    rev = lambda i, g: (i, rev_block(g), 0)
    fwd4 = lambda i, g: (i, g, 0, 0)
    rev4 = lambda i, g: (i, rev_block(g), 0, 0)
    tokf = pl.BlockSpec((1, SCAN_CHUNKS * CHUNK, WIDTH), fwd)
    tokr = pl.BlockSpec((1, SCAN_CHUNKS * CHUNK, WIDTH), rev)
    ecf = pl.BlockSpec((1, SCAN_CHUNKS, 1, WIDTH), fwd4)
    ecr = pl.BlockSpec((1, SCAN_CHUNKS, 1, WIDTH), rev4)
    y_shape = jax.ShapeDtypeStruct((b, t_tot, WIDTH), F32)
    return pl.pallas_call(
        _scan_kernel,
        grid=(b, n_blocks),
        in_specs=[tokf] * 4 + [ecf, tokf] + [tokr] * 4 + [ecr, tokr],
        out_specs=[tokf, tokr],
        out_shape=[y_shape, y_shape],
        scratch_shapes=[pltpu.VMEM((2, N_PAIRS, 2 * HEAD_DIM, 2 * HEAD_DIM), F32)],
        compiler_params=_params(("parallel", "arbitrary")),
        name="rwkv_scan",
    )(at0, kt0, bt0, rt0, ec0, v, at1, kt1, bt1, rt1, ec1, v)


NA_ROWS = 32


def _na_kernel(ctx_len, n_rows, q_ref, k_ref, v_ref, bias_ref, o_ref):
    step = pl.program_id(2)
    win = NA_KH * GRID_W
    kc = k_ref[0, 0:ctx_len, :]
    vc = v_ref[0, 0:ctx_len, :]
    lo = lax.broadcasted_iota(jnp.int32, (GRID_W, LANES), 1) < HEAD_DIM
    rows = range(NA_ROWS)
    r0 = [jnp.clip(step * NA_ROWS + j - NA_KH // 2, 0, n_rows - NA_KH) for j in rows]
    delta = [step * NA_ROWS + j - r for j, r in zip(rows, r0)]
    start = [pl.multiple_of(ctx_len + r * GRID_W, GRID_W) for r in r0]
    qst = [_stack(q_ref[0, j * GRID_W:(j + 1) * GRID_W, :], lo) for j in rows]
    s_loc = [_dot_nt(q, k_ref[0, pl.ds(s, win), :]) + bias_ref[dl, 0] for q, s, dl in zip(qst, start, delta)]
    s_ctx = [_dot_nt(q, kc) for q in qst]
    mx = [jnp.maximum(jnp.max(a, axis=-1, keepdims=True), jnp.max(c, axis=-1, keepdims=True))
          for a, c in zip(s_loc, s_ctx)]
    p_loc = [jnp.exp2(a - m).astype(BF16) for a, m in zip(s_loc, mx)]
    p_ctx = [jnp.exp2(c - m).astype(BF16) for c, m in zip(s_ctx, mx)]
    ones_loc = jnp.ones((win, LANES), BF16)
    ones_ctx = jnp.ones((ctx_len, LANES), BF16)
    vc1 = jnp.concatenate([vc, ones_ctx], axis=1)
    od = [_dot(a, jnp.concatenate([v_ref[0, pl.ds(s, win), :], ones_loc], axis=1)) + _dot(c, vc1)
          for a, c, s in zip(p_loc, p_ctx, start)]
    for j, x in zip(rows, od):
        oj = x[:, :LANES] / x[:, LANES:]
        o_ref[0, j * GRID_W:(j + 1) * GRID_W, :] = jnp.where(lo, oj[:GRID_W], oj[GRID_W:]).astype(o_ref.dtype)


def _neighborhood_attention(qg, kv, bias, ctx_len):
    b, t_tot, _ = kv.shape
    t = t_tot - ctx_len
    n_rows = t // GRID_W
    assert n_rows % NA_ROWS == 0
    blk_rows = NA_ROWS * GRID_W
    return pl.pallas_call(
        functools.partial(_na_kernel, ctx_len, n_rows),
        grid=(b, N_PAIRS, n_rows // NA_ROWS),
        in_specs=[pl.BlockSpec((1, blk_rows, LANES), lambda bi, pr, i: (bi, i, pr)),
                  pl.BlockSpec((1, t_tot, LANES), lambda bi, pr, i: (bi, 0, pr)),
                  pl.BlockSpec((1, t_tot, LANES), lambda bi, pr, i: (bi, 0, N_PAIRS + pr)),
                  pl.BlockSpec((NA_KH, 1, 2 * GRID_W, NA_KH * GRID_W), lambda bi, pr, i: (0, pr, 0, 0))],
        out_specs=pl.BlockSpec((1, blk_rows, LANES), lambda bi, pr, i: (bi, i, pr)),
        out_shape=jax.ShapeDtypeStruct((b, t, WIDTH), BF16),
        compiler_params=_params(("parallel", "parallel", "arbitrary")),
        name="neighborhood_attention",
    )(qg, kv, kv, bias)


def _na_bias_table(na_rpb):
    col = np.arange(GRID_W)
    c0 = np.clip(col - NA_KW // 2, 0, GRID_W - NA_KW)
    valid = (col[None, :] >= c0[:, None]) & (col[None, :] < c0[:, None] + NA_KW)
    col_off = np.clip(col[None, :] - col[:, None] + NA_KW - 1, 0, 2 * NA_KW - 2)
    row_off = np.arange(NA_KH)[None, :] - np.arange(NA_KH)[:, None] + NA_KH - 1
    sel_r = (row_off[..., None] == np.arange(2 * NA_KH - 1)).astype(np.float32)
    sel_c = ((col_off[..., None] == np.arange(2 * NA_KW - 1)) & valid[..., None]).astype(np.float32)
    tbl = jnp.einsum('dri,hij,qcj->dhqrc', sel_r, na_rpb.astype(F32), sel_c, precision=lax.Precision.HIGHEST)
    tbl = tbl * LOG2E + jnp.asarray(np.where(valid, 0.0, NEG_BIG).astype(np.float32))[None, None, :, None, :]
    return tbl.reshape(NA_KH, N_PAIRS, 2 * GRID_W, NA_KH * GRID_W)


def _out_kernel(yf_ref, yr_ref, bonus_ref, na_ref, nag_ref, rwg_ref, x_ref, gate_ref, wo_ref,
                gnw_ref, gnb_ref, fg_ref, seg_ref, o_ref):
    seg = seg_ref[...]
    y = yf_ref[0] + yr_ref[0]
    inv_n = 1.0 / HEAD_DIM
    mu = _split_dot_right(y, seg, 2) * inv_n
    dev = y - mu
    var = _split_dot_right(dev * dev, seg, 2) * inv_n
    yn = dev * lax.rsqrt(var + GN_EPS) * gnw_ref[...] + gnb_ref[...]
    rwg = rwg_ref[0].astype(F32)
    nag = nag_ref[0].astype(F32)
    rw = (yn + bonus_ref[0]) * (rwg * _sigmoid(rwg))
    na = na_ref[0].astype(F32) * (nag * _sigmoid(nag))
    mix = jnp.concatenate([na, rw], axis=-1).astype(BF16)
    h = x_ref[0] + gate_ref[0] * _dot(mix, wo_ref[...])
    ms = jnp.mean(h * h, axis=-1, keepdims=True)
    o_ref[0] = h * lax.rsqrt(ms + RMS_EPS) * fg_ref[...]


def _readout_project(yf, yr, bonus, na, qg, x, gate, w_out_bf16, gn_w, gn_b, final_g, seg, n_ctx_tiles):
    b, t, d = x.shape
    lat = lambda i, j: (i, j, 0)
    off = lambda i, j: (i, j + n_ctx_tiles, 0)
    fixed2 = lambda i, j: (0, 0)
    tok_off = pl.BlockSpec((1, TILE, WIDTH), off)
    return pl.pallas_call(
        _out_kernel,
        grid=(b, t // TILE),
        in_specs=[tok_off, tok_off, tok_off,
                  pl.BlockSpec((1, TILE, WIDTH), lat),
                  pl.BlockSpec((1, TILE, WIDTH), lambda i, j: (i, j, 1)),
                  pl.BlockSpec((1, TILE, WIDTH), lambda i, j: (i, j, 2)),
                  pl.BlockSpec((1, TILE, d), lat),
                  pl.BlockSpec((1, 1, d), lambda i, j: (i, 0, 0)),
                  pl.BlockSpec((2 * WIDTH, d), fixed2),
                  pl.BlockSpec((1, WIDTH), fixed2),
                  pl.BlockSpec((1, WIDTH), fixed2),
                  pl.BlockSpec((1, d), fixed2),
                  pl.BlockSpec((WIDTH, WIDTH), fixed2)],
        out_specs=pl.BlockSpec((1, TILE, d), lat),
        out_shape=jax.ShapeDtypeStruct((b, t, d), F32),
        compiler_params=_params(("parallel", "arbitrary")),
        name="readout_project",
    )(yf, yr, bonus, na, qg, qg, x, gate, w_out_bf16, gn_w, gn_b, final_g, seg)


def _chunk_matrices():
    idx = np.arange(TILE)
    same = (idx[:, None] // CHUNK) == (idx[None, :] // CHUNK)
    tri_f = same & (idx[:, None] >= idx[None, :])
    tri_r = same & (idx[:, None] <= idx[None, :])
    tri = jnp.asarray(np.stack([tri_f, tri_r]).astype(np.float32), dtype=BF16)
    hid = np.arange(WIDTH) // HEAD_DIM
    seg = jnp.asarray((hid[:, None] == hid[None, :]).astype(np.float32), dtype=BF16)
    return tri, seg


def _lora_cat(w2):
    z = jnp.zeros_like(w2[0])
    return jnp.stack([jnp.concatenate([w2[0], z], axis=0), jnp.concatenate([z, w2[1]], axis=0)]).astype(BF16)


def kernel(x, c, ctx, c_ctx, w_mod, b_mod, norm_g, w_in, conv_w, decay_w0, decay_w2, aaa_a0, aaa_a2,
           k_k, k_a, r_k, gn_w, gn_b, na_rpb, w_out, final_g):
    depth = w_mod.shape[0]
    assert depth == 1, "single-layer block: the context stream is never updated"
    b, t, d = x.shape
    l = ctx.shape[1]
    assert l == TILE and t % TILE == 0 and t // GRID_W >= NA_KH and w_in.shape[2] == D_IN
    n_ctx_tiles = l // TILE

    rows = -(-(b + 1) // 8) * 8
    cvecs = jnp.zeros((rows, d), F32).at[:b].set(c).at[b].set(c_ctx)
    mod = _modulation(cvecs, w_mod[0], b_mod[0])
    shift, scale, gate = mod[:, :d], mod[:, d:2 * d], mod[:, 2 * d:]

    tri, seg = _chunk_matrices()
    kv, qg, *prep = _project_and_prepare(
        x, ctx, shift[:b].reshape(b, 1, d), scale[:b].reshape(b, 1, d), shift[b:b + 1], scale[b:b + 1],
        norm_g[0].reshape(1, d), w_in[0].astype(BF16), conv_w[0], decay_w0[0], _lora_cat(decay_w2[0]),
        aaa_a0[0], _lora_cat(aaa_a2[0]), k_k[0].reshape(1, WIDTH), k_a[0].reshape(1, WIDTH),
        r_k[0].reshape(1, WIDTH), seg, tri)
    yf, yr = _rwkv_scan(prep, l // CHUNK)
    na = _neighborhood_attention(qg, kv, _na_bias_table(na_rpb[0]), l)
    return _readout_project(yf, yr, prep[-1], na, qg, x, gate[:b].reshape(b, 1, d), w_out[0].astype(BF16),
                            gn_w[0].reshape(1, WIDTH), gn_b[0].reshape(1, WIDTH), final_g.reshape(1, d), seg,
                            n_ctx_tiles)
```

```python
import functools

import numpy as np
import jax
import jax.numpy as jnp
from jax import lax
from jax.experimental import pallas as pl
from jax.experimental.pallas import tpu as pltpu

F32 = jnp.float32
BF16 = jnp.bfloat16

HEAD_DIM = 64
LANES = 128
NA_HEADS = 8
RW_HEADS = 8
WIDTH = NA_HEADS * HEAD_DIM
N_PAIRS = WIDTH // LANES
GRID_W = 64
NA_KH = 8
NA_KW = 16
LORA = 64
CHUNK = 64
TILE = 256
RMS_EPS = 1e-6
GN_EPS = 64e-5
NEG_BIG = -1e30
LOG2E = 1.4426950408889634
VMEM_LIMIT = 56 * 1024 * 1024

O_NA_K = 0
O_NA_V = WIDTH
O_RW_K = 2 * WIDTH
CONV_W = 3 * WIDTH + 4 * LORA
O_NA_Q = O_RW_K + CONV_W
D_IN = O_NA_Q + 3 * WIDTH


def _dot(a, b):
    return jnp.dot(a, b, preferred_element_type=F32)


def _dot_nt(a, b):
    return lax.dot_general(a, b, (((1,), (1,)), ((), ())), preferred_element_type=F32)


def _dot_tn(a, b):
    return lax.dot_general(a, b, (((0,), (0,)), ((), ())), preferred_element_type=F32)


def _split_dot(m, x, terms):
    acc = None
    rem = x
    for _ in range(terms):
        piece = rem.astype(BF16)
        part = _dot(m, piece)
        acc = part if acc is None else acc + part
        rem = rem - piece.astype(F32)
    return acc


def _split_dot_right(x, m, terms):
    acc = None
    rem = x
    for _ in range(terms):
        piece = rem.astype(BF16)
        part = _dot(piece, m)
        acc = part if acc is None else acc + part
        rem = rem - piece.astype(F32)
    return acc


def _sigmoid(z):
    return 1.0 / (1.0 + jnp.exp2(z * (-LOG2E)))


def _params(sem):
    return pltpu.CompilerParams(dimension_semantics=sem, vmem_limit_bytes=VMEM_LIMIT)


def _mod_kernel(c_ref, w_ref, b_ref, o_ref):
    cv = c_ref[...]
    o_ref[...] = _dot(cv * _sigmoid(cv), w_ref[...]) + b_ref[...]


def _modulation(cvecs, w_mod, b_mod):
    rows, d = cvecs.shape
    n_out = w_mod.shape[1]
    blk = d
    return pl.pallas_call(
        _mod_kernel,
        grid=(n_out // blk,),
        in_specs=[pl.BlockSpec((rows, d), lambda j: (0, 0)),
                  pl.BlockSpec((d, blk), lambda j: (0, j)),
                  pl.BlockSpec((1, blk), lambda j: (0, j))],
        out_specs=pl.BlockSpec((rows, blk), lambda j: (0, j)),
        out_shape=jax.ShapeDtypeStruct((rows, n_out), F32),
        compiler_params=_params(("arbitrary",)),
        name="modulation",
    )(cvecs, w_mod, b_mod.reshape(1, n_out))


_PROJ_COLS = 256


def _proj_prep_kernel(n_ctx_tiles, x_ref, xprev_ref, xnext_ref, ctx_ref, sh_ref, sc_ref, shc_ref, scc_ref, g_ref,
                      w_ref, cw_ref, w0_ref, w2_ref, a0_ref, a2_ref, kk_w_ref, ka_ref, rk_ref, seg_ref, tri_ref,
                      kv_ref, qg_ref, at0, kt0, bt0, rt0, ec0, at1, kt1, bt1, rt1, ec1, v_ref,
                      bonus_ref, cv_scr, halo_scr):
    tile = pl.program_id(1)
    n_tiles = pl.num_programs(1)
    is_ctx = tile < n_ctx_tiles
    gain = g_ref[...]

    def normed(xin, sh, sc):
        ms = jnp.mean(xin * xin, axis=-1, keepdims=True)
        return ((xin * lax.rsqrt(ms + RMS_EPS) * gain) * (1.0 + sc) + sh).astype(BF16)

    xb = normed(jnp.where(is_ctx, ctx_ref[0], x_ref[0]), jnp.where(is_ctx, shc_ref[...], sh_ref[0]),
                jnp.where(is_ctx, scc_ref[...], sc_ref[0]))

    def project(store, col0, j, scaled):
        res = _dot(xb, w_ref[:, col0 + j:col0 + j + _PROJ_COLS])
        if scaled:
            res = res * (HEAD_DIM ** -0.5 * LOG2E)
        store(slice(j, j + _PROJ_COLS), res.astype(BF16))

    def chunks(store, col0, width, scale_cols=0):
        return [functools.partial(project, store, col0, j, j < scale_cols) for j in range(0, width, _PROJ_COLS)]

    def to_kv(cols, val):
        kv_ref[0, :, cols] = val

    def to_cv(cols, val):
        cv_scr[:, cols] = val

    def to_qg(cols, val):
        qg_ref[0, :, cols] = val

    conv_chunks = chunks(to_cv, O_RW_K, CONV_W)
    pending = chunks(to_kv, O_NA_K, 2 * WIDTH) + chunks(to_qg, O_NA_Q, 3 * WIDTH, scale_cols=WIDTH)

    def more(queue=pending, n=1):
        for _ in range(n):
            if queue:
                queue.pop(0)()

    hb = normed(jnp.concatenate([xprev_ref[0], xnext_ref[0]], axis=0), sh_ref[0], sc_ref[0])
    halo_scr[...] = _dot(hb, w_ref[:, O_RW_K:O_RW_K + CONV_W]).astype(BF16).astype(F32)

    tm = cv_scr.shape[0]
    first = jnp.logical_or(tile == 0, tile == n_ctx_tiles)
    last = jnp.logical_or(tile == n_ctx_tiles - 1, tile == n_tiles - 1)
    row8 = lax.broadcasted_iota(jnp.int32, (8, 1), 0)

    def conv(lo_col, hi_col):
        cols = slice(lo_col, hi_col)
        p = cv_scr[:, cols].astype(F32)
        w_prev, w_mid, w_next = cw_ref[0:1, cols], cw_ref[1:2, cols], cw_ref[2:3, cols]
        u = pltpu.roll(p, 1, axis=0) * w_prev + p * w_mid + pltpu.roll(p, tm - 1, axis=0) * w_next
        prev_row = jnp.where(first, 0.0, halo_scr[7:8, cols])
        next_row = jnp.where(last, 0.0, halo_scr[8:9, cols])
        top = u[0:8] + jnp.where(row8 == 0, (prev_row - p[tm - 1:tm]) * w_prev, 0.0)
        bot = u[tm - 8:] + jnp.where(row8 == 7, (next_row - p[0:1]) * w_next, 0.0)
        return jnp.concatenate([top, u[8:tm - 8], bot], axis=0)

    per_group = WIDTH // _PROJ_COLS
    more(conv_chunks, per_group)
    k = conv(0, WIDTH)
    more(conv_chunks, per_group)
    v = conv(WIDTH, 2 * WIDTH)
    more(conv_chunks, 4 * LORA // _PROJ_COLS)
    lora_in = conv(2 * WIDTH, 2 * WIDTH + 4 * LORA)
    wd = jnp.tanh(lora_in[:, :2 * LORA]).astype(BF16)
    ad = lora_in[:, 2 * LORA:].astype(BF16)
    more(conv_chunks, per_group)
    assert not conv_chunks
    r = conv(2 * WIDTH + 4 * LORA, CONV_W)
    seg = seg_ref[...]

    kk = k * kk_w_ref[...]
    kk = kk * lax.rsqrt(jnp.maximum(_split_dot_right(kk * kk, seg, 2), 1e-24))
    more()
    v_ref[0] = v.astype(BF16)
    k_ka = k * ka_ref[...]
    k_rest = k - k_ka
    r_rk = r * rk_ref[...]
    lw_scale = np.float32(-np.exp(-0.5) * LOG2E)

    kd_sum = None
    outs = ((at0, kt0, bt0, rt0, ec0), (at1, kt1, bt1, rt1, ec1))
    for d, (at, kt, bt, rt, ec) in enumerate(outs):
        lw = lw_scale * _sigmoid(w0_ref[d:d + 1, :] + _dot(wd, w2_ref[d]))
        more()
        a = _sigmoid(a0_ref[d:d + 1, :] + _dot(ad, a2_ref[d]))
        kd = k_rest + k_ka * a
        bb = kk * a
        kd_sum = kd if kd_sum is None else kd_sum + kd
        more()
        cs = _split_dot(tri_ref[d], lw, 2)
        e_inv = jnp.exp2(-cs)
        at[0] = (kk * jnp.exp2(cs - lw)).astype(BF16)
        more()
        kt[0] = (kd * e_inv).astype(BF16)
        bt[0] = (bb * e_inv).astype(BF16)
        more()
        rt[0] = (r * jnp.exp2(cs)).astype(BF16)
        for c in range(tm // CHUNK):
            end = c * CHUNK + (CHUNK - 1 if d == 0 else 0)
            ec[0, c] = jnp.exp2(cs[end:end + 1, :])
    bonus_ref[0] = (_split_dot_right(r_rk * kd_sum, seg, 2) * v).astype(bonus_ref.dtype)
    while pending:
        more()


def _project_and_prepare(x, ctx, shift_m, scale_m, shift_c, scale_c, norm_g, w_in_bf16,
                         conv_w, w0, w2cat, a0, a2cat, k_k, k_a, r_k, seg, tri):
    b, t, d = x.shape
    l = ctx.shape[1]
    n_ctx_tiles = l // TILE
    n_tiles = n_ctx_tiles + t // TILE
    t_tot = l + t
    rows8 = TILE // 8
    lat = lambda i, j: (i, jnp.maximum(j - n_ctx_tiles, 0), 0)
    prev = lambda i, j: (i, jnp.maximum((j - n_ctx_tiles) * rows8 - 1, 0), 0)
    nxt = lambda i, j: (i, jnp.clip((j - n_ctx_tiles + 1) * rows8, 0, t // 8 - 1), 0)
    cidx = lambda i, j: (i, jnp.minimum(j, n_ctx_tiles - 1), 0)
    per_b = lambda i, j: (i, 0, 0)
    fixed2 = lambda i, j: (0, 0)
    fixed3 = lambda i, j: (0, 0, 0)
    out_idx = lambda i, j: (i, j, 0)
    tok = pl.BlockSpec((1, TILE, WIDTH), out_idx)
    ecs = pl.BlockSpec((1, TILE // CHUNK, 1, WIDTH), lambda i, j: (i, j, 0, 0))
    tok_shape = jax.ShapeDtypeStruct((b, t_tot, WIDTH), BF16)
    ec_shape = jax.ShapeDtypeStruct((b, t_tot // CHUNK, 1, WIDTH), F32)
    per_dir_specs = [tok] * 4 + [ecs]
    per_dir_shapes = [tok_shape] * 4 + [ec_shape]
    return pl.pallas_call(
        functools.partial(_proj_prep_kernel, n_ctx_tiles),
        grid=(b, n_tiles),
        in_specs=[pl.BlockSpec((1, TILE, d), lat),
                  pl.BlockSpec((1, 8, d), prev),
                  pl.BlockSpec((1, 8, d), nxt),
                  pl.BlockSpec((1, TILE, d), cidx),
                  pl.BlockSpec((1, 1, d), per_b),
                  pl.BlockSpec((1, 1, d), per_b),
                  pl.BlockSpec((1, d), fixed2),
                  pl.BlockSpec((1, d), fixed2),
                  pl.BlockSpec((1, d), fixed2),
                  pl.BlockSpec((d, D_IN), fixed2),
                  pl.BlockSpec((3, CONV_W), fixed2),
                  pl.BlockSpec((2, WIDTH), fixed2),
                  pl.BlockSpec((2, 2 * LORA, WIDTH), fixed3),
                  pl.BlockSpec((2, WIDTH), fixed2),
                  pl.BlockSpec((2, 2 * LORA, WIDTH), fixed3),
                  pl.BlockSpec((1, WIDTH), fixed2),
                  pl.BlockSpec((1, WIDTH), fixed2),
                  pl.BlockSpec((1, WIDTH), fixed2),
                  pl.BlockSpec((WIDTH, WIDTH), fixed2),
                  pl.BlockSpec((2, TILE, TILE), fixed3)],
        out_specs=[pl.BlockSpec((1, TILE, 2 * WIDTH), out_idx),
                   pl.BlockSpec((1, TILE, 3 * WIDTH), lat)] + per_dir_specs * 2
                  + [tok, pl.BlockSpec((1, TILE, WIDTH), lat)],
        out_shape=[jax.ShapeDtypeStruct((b, t_tot, 2 * WIDTH), BF16),
                   jax.ShapeDtypeStruct((b, t, 3 * WIDTH), BF16)] + per_dir_shapes * 2
                  + [tok_shape, jax.ShapeDtypeStruct((b, t, WIDTH), BF16)],
        scratch_shapes=[pltpu.VMEM((TILE, CONV_W), BF16), pltpu.VMEM((16, CONV_W), F32)],
        compiler_params=_params(("parallel", "arbitrary")),
        name="project_and_prepare",
    )(x, x, x, ctx, shift_m, scale_m, shift_c, scale_c, norm_g, w_in_bf16,
      conv_w, w0, w2cat, a0, a2cat, k_k, k_a, r_k, seg, tri)


_INV_SQUARINGS = 5


def _stack(xp, lo):
    zero = jnp.zeros_like(xp)
    return jnp.concatenate([jnp.where(lo, xp, zero), jnp.where(lo, zero, xp)], axis=0)


def _chunk_operators(units, masks):
    lo, strict_f, incl_f, strict_r, incl_r, eye_w, eye_sq, same_head = masks
    cat = jnp.concatenate
    c = CHUNK
    b16 = lambda xs: [x.astype(BF16) for x in xs]
    stack = lambda xs: [_stack(x, lo) for x in xs]
    strict = [strict_r if u["rev"] else strict_f for u in units]
    incl = [incl_r if u["rev"] else incl_f for u in units]
    a_t, k_t, b_t, r_t, v = ([u[n] for u in units] for n in ("a_t", "k_t", "b_t", "r_t", "v"))
    ast, kst, bst, vst = stack(a_t), stack(k_t), stack(b_t), stack(v)
    scores = [_dot_nt(cat([a, r], axis=0), cat([ks, bs], axis=0)) for a, r, ks, bs in zip(a_t, r_t, kst, bst)]
    ak = b16([jnp.where(mk, sc[:c, :LANES], 0.0) for mk, sc in zip(strict, scores)])
    nm = [jnp.where(mk, -sc[:c, LANES:], 0.0) for mk, sc in zip(strict, scores)]
    rkb = b16([jnp.where(mk, sc[c:, :], 0.0) for mk, sc in zip(incl, scores)])
    w1st = stack(b16([_dot(x, vs) for x, vs in zip(ak, vst)]))
    nmb = b16(nm)
    pw = [_dot(x, xs) for x, xs in zip(nmb, stack(nmb))]
    tinv = [eye_w + x for x in nm]
    for _ in range(_INV_SQUARINGS - 1):
        pwb = b16(pw)
        prod = [_dot(cat([pb, t.astype(BF16)], axis=0), ps) for pb, t, ps in zip(pwb, tinv, stack(pwb))]
        pw = [x[:c] for x in prod]
        tinv = [t + x[c:] for t, x in zip(tinv, prod)]
    tinv = [t + _dot(t.astype(BF16), ps) for t, ps in zip(tinv, stack(b16(pw)))]
    gu = b16([_dot(t.astype(BF16), cat([x, w], axis=1)) for t, x, w in zip(tinv, ast, w1st)])
    g = [x[:, :LANES] for x in gu]
    u0 = [x[:, LANES:] for x in gu]
    q = b16([r.astype(F32) - _dot(rb[:, LANES:], gs) for r, rb, gs in zip(r_t, rkb, stack(g))])
    y0 = [_dot(rb, cat([vs, -us], axis=0)) for rb, vs, us in zip(rkb, vst, stack(u0))]
    k_h = b16([x.astype(F32) * u["e_c"] for x, u in zip(k_t, units)])
    b_h = b16([x.astype(F32) * u["e_c"] for x, u in zip(b_t, units)])
    m = b16([jnp.where(same_head, jnp.where(eye_sq, u["e_c"], 0.0) - _dot_tn(bh, gg), 0.0)
             for u, bh, gg in zip(units, b_h, g)])
    dd = [jnp.where(same_head, _dot_tn(cat([kh, bh], axis=0), cat([vv, -uu], axis=0)), 0.0)
          for kh, bh, vv, uu in zip(k_h, b_h, v, u0)]
    return [(cat([qq, mm], axis=0), yy, d2) for qq, mm, yy, d2 in zip(q, m, y0, dd)]


SCAN_CHUNKS = 4


def _scan_kernel(at0, kt0, bt0, rt0, ec0, v0, at1, kt1, bt1, rt1, ec1, v1, y0_ref, y1_ref, s_ref):
    @pl.when(pl.program_id(1) == 0)
    def _():
        s_ref[...] = jnp.zeros_like(s_ref)

    c = CHUNK
    lane_w = lax.broadcasted_iota(jnp.int32, (c, LANES), 1)
    t_w = lax.broadcasted_iota(jnp.int32, (c, LANES), 0)
    s_w = lane_w & (c - 1)
    t_w2 = lax.broadcasted_iota(jnp.int32, (c, 2 * LANES), 0)
    s_w2 = lax.broadcasted_iota(jnp.int32, (c, 2 * LANES), 1) & (c - 1)
    ri = lax.broadcasted_iota(jnp.int32, (LANES, LANES), 0)
    ci = lax.broadcasted_iota(jnp.int32, (LANES, LANES), 1)
    masks = (lane_w < HEAD_DIM,
             t_w > s_w, t_w2 >= s_w2, t_w < s_w, t_w2 <= s_w2,
             jnp.where(t_w == s_w, 1.0, 0.0), ri == ci, (ri // HEAD_DIM) == (ci // HEAD_DIM))
    dirs = ((at0, kt0, bt0, rt0, ec0, v0, y0_ref), (at1, kt1, bt1, rt1, ec1, v1, y1_ref))
    units = []
    where = []
    for j in range(SCAN_CHUNKS):
        for d, (at, kt, bt, rt, ec, vv, y_ref) in enumerate(dirs):
            pos = j if d == 0 else SCAN_CHUNKS - 1 - j
            rows = slice(pos * c, (pos + 1) * c)
            for pr in range(N_PAIRS):
                sl = slice(pr * LANES, (pr + 1) * LANES)
                units.append(dict(rev=d == 1, a_t=at[0, rows, sl], k_t=kt[0, rows, sl], b_t=bt[0, rows, sl],
                                  r_t=rt[0, rows, sl], v=vv[0, rows, sl], e_c=ec[0, pos, :, sl]))
                where.append((d, pr, y_ref, rows, sl))
    ops = _chunk_operators(units, masks)
    n_seq = 2 * N_PAIRS
    state = [s_ref[d, pr] for (d, pr, _, _, _) in where[:n_seq]]
    for j in range(SCAN_CHUNKS):
        step_ops = ops[j * n_seq:(j + 1) * n_seq]
        prods = [_dot(qm, s.astype(BF16)) for (qm, _, _), s in zip(step_ops, state)]
        for x, (_, y0, _), (_, _, y_ref, rows, sl) in zip(prods, step_ops, where[j * n_seq:(j + 1) * n_seq]):
            y_ref[0, rows, sl] = (x[:c] + y0).astype(y_ref.dtype)
        state = [x[c:] + dd for x, (_, _, dd) in zip(prods, step_ops)]
    for s, (d, pr, _, _, _) in zip(state, where[:n_seq]):
        s_ref[d, pr] = s


def _rwkv_scan(prep, n_ctx_chunks):
    (at0, kt0, bt0, rt0, ec0, at1, kt1, bt1, rt1, ec1, v, _) = prep
    b, t_tot, _ = v.shape
    assert t_tot % (SCAN_CHUNKS * CHUNK) == 0 and n_ctx_chunks % SCAN_CHUNKS == 0
    n_blocks = t_tot // (SCAN_CHUNKS * CHUNK)
    n_ctx_blocks = n_ctx_chunks // SCAN_CHUNKS

    def rev_block(g):
        return jnp.where(g < n_ctx_blocks, n_ctx_blocks - 1 - g, n_blocks - 1 + n_ctx_blocks - g)

    fwd = lambda i, g: (i, g, 0)
    rev = lambda i, g: (i, rev_block(g), 0)
    fwd4 = lambda i, g: (i, g, 0, 0)
    rev4 = lambda i, g: (i, rev_block(g), 0, 0)
    tokf = pl.BlockSpec((1, SCAN_CHUNKS * CHUNK, WIDTH), fwd)
    tokr = pl.BlockSpec((1, SCAN_CHUNKS * CHUNK, WIDTH), rev)
    ecf = pl.BlockSpec((1, SCAN_CHUNKS, 1, WIDTH), fwd4)
    ecr = pl.BlockSpec((1, SCAN_CHUNKS, 1, WIDTH), rev4)
    blk_rows = SCAN_CHUNKS * CHUNK
    y_fwd = pl.BlockSpec((1, blk_rows, WIDTH), lambda i, g: (i, jnp.maximum(g - n_ctx_blocks, 0), 0))
    y_rev = pl.BlockSpec((1, blk_rows, WIDTH),
                         lambda i, g: (i, rev_block(jnp.maximum(g, n_ctx_blocks)) - n_ctx_blocks, 0))
    y_shape = jax.ShapeDtypeStruct((b, t_tot - n_ctx_chunks * CHUNK, WIDTH), BF16)
    return pl.pallas_call(
        _scan_kernel,
        grid=(b, n_blocks),
        in_specs=[tokf] * 4 + [ecf, tokf] + [tokr] * 4 + [ecr, tokr],
        out_specs=[y_fwd, y_rev],
        out_shape=[y_shape, y_shape],
        scratch_shapes=[pltpu.VMEM((2, N_PAIRS, 2 * HEAD_DIM, 2 * HEAD_DIM), F32)],
        compiler_params=_params(("parallel", "arbitrary")),
        name="rwkv_scan",
    )(at0, kt0, bt0, rt0, ec0, v, at1, kt1, bt1, rt1, ec1, v)


NA_ROWS = 32


def _na_kernel(ctx_len, n_rows, q_ref, k_ref, v_ref, bias_ref, o_ref):
    step = pl.program_id(2)
    win = NA_KH * GRID_W
    kc = k_ref[0, 0:ctx_len, :]
    vc = v_ref[0, 0:ctx_len, :]
    lo = lax.broadcasted_iota(jnp.int32, (GRID_W, LANES), 1) < HEAD_DIM
    rows = range(NA_ROWS)
    r0 = [jnp.clip(step * NA_ROWS + j - NA_KH // 2, 0, n_rows - NA_KH) for j in rows]
    delta = [step * NA_ROWS + j - r for j, r in zip(rows, r0)]
    start = [pl.multiple_of(ctx_len + r * GRID_W, GRID_W) for r in r0]
    qst = [_stack(q_ref[0, j * GRID_W:(j + 1) * GRID_W, :], lo) for j in rows]
    s_loc = [_dot_nt(q, k_ref[0, pl.ds(s, win), :]) + bias_ref[dl, 0] for q, s, dl in zip(qst, start, delta)]
    s_ctx = [_dot_nt(q, kc) for q in qst]
    mx = [jnp.maximum(jnp.max(a, axis=-1, keepdims=True), jnp.max(c, axis=-1, keepdims=True))
          for a, c in zip(s_loc, s_ctx)]
    p_loc = [jnp.exp2(a - m).astype(BF16) for a, m in zip(s_loc, mx)]
    p_ctx = [jnp.exp2(c - m).astype(BF16) for c, m in zip(s_ctx, mx)]
    ones_loc = jnp.ones((win, LANES), BF16)
    ones_ctx = jnp.ones((ctx_len, LANES), BF16)
    vc1 = jnp.concatenate([vc, ones_ctx], axis=1)
    od = [_dot(a, jnp.concatenate([v_ref[0, pl.ds(s, win), :], ones_loc], axis=1)) + _dot(c, vc1)
          for a, c, s in zip(p_loc, p_ctx, start)]
    for j, x in zip(rows, od):
        oj = x[:, :LANES] / x[:, LANES:]
        o_ref[0, j * GRID_W:(j + 1) * GRID_W, :] = jnp.where(lo, oj[:GRID_W], oj[GRID_W:]).astype(o_ref.dtype)


def _neighborhood_attention(qg, kv, bias, ctx_len):
    b, t_tot, _ = kv.shape
    t = t_tot - ctx_len
    n_rows = t // GRID_W
    assert n_rows % NA_ROWS == 0
    blk_rows = NA_ROWS * GRID_W
    return pl.pallas_call(
        functools.partial(_na_kernel, ctx_len, n_rows),
        grid=(b, N_PAIRS, n_rows // NA_ROWS),
        in_specs=[pl.BlockSpec((1, blk_rows, LANES), lambda bi, pr, i: (bi, i, pr)),
                  pl.BlockSpec((1, t_tot, LANES), lambda bi, pr, i: (bi, 0, pr)),
                  pl.BlockSpec((1, t_tot, LANES), lambda bi, pr, i: (bi, 0, N_PAIRS + pr)),
                  pl.BlockSpec((NA_KH, 1, 2 * GRID_W, NA_KH * GRID_W), lambda bi, pr, i: (0, pr, 0, 0))],
        out_specs=pl.BlockSpec((1, blk_rows, LANES), lambda bi, pr, i: (bi, i, pr)),
        out_shape=jax.ShapeDtypeStruct((b, t, WIDTH), BF16),
        compiler_params=_params(("parallel", "parallel", "arbitrary")),
        name="neighborhood_attention",
    )(qg, kv, kv, bias)


def _na_bias_table(na_rpb):
    col = np.arange(GRID_W)
    c0 = np.clip(col - NA_KW // 2, 0, GRID_W - NA_KW)
    valid = (col[None, :] >= c0[:, None]) & (col[None, :] < c0[:, None] + NA_KW)
    col_off = np.clip(col[None, :] - col[:, None] + NA_KW - 1, 0, 2 * NA_KW - 2)
    row_off = np.arange(NA_KH)[None, :] - np.arange(NA_KH)[:, None] + NA_KH - 1
    n_col = 2 * NA_KW - 1
    sel_r = (row_off[..., None] == np.arange(2 * NA_KH - 1)).astype(np.float32)
    sel_c = (np.where(valid, col_off, n_col)[..., None] == np.arange(n_col + 1)).astype(np.float32)
    src = jnp.concatenate([na_rpb.astype(F32) * LOG2E, jnp.full(na_rpb.shape[:2] + (1,), NEG_BIG, F32)], axis=-1)
    tbl = jnp.einsum('dri,hij,qcj->dhqrc', sel_r, src, sel_c, precision=lax.Precision.HIGHEST)
    return tbl.reshape(NA_KH, N_PAIRS, 2 * GRID_W, NA_KH * GRID_W)


def _out_kernel(yf_ref, yr_ref, bonus_ref, na_ref, nag_ref, rwg_ref, x_ref, gate_ref, wo_ref,
                gnw_ref, gnb_ref, fg_ref, seg_ref, o_ref):
    seg = seg_ref[...]
    y = yf_ref[0].astype(F32) + yr_ref[0].astype(F32)
    inv_n = 1.0 / HEAD_DIM
    mu = _split_dot_right(y, seg, 2) * inv_n
    dev = y - mu
    var = _split_dot_right(dev * dev, seg, 2) * inv_n
    yn = dev * lax.rsqrt(var + GN_EPS) * gnw_ref[...] + gnb_ref[...]
    rwg = rwg_ref[0].astype(F32)
    nag = nag_ref[0].astype(F32)
    rw = (yn + bonus_ref[0].astype(F32)) * (rwg * _sigmoid(rwg))
    na = na_ref[0].astype(F32) * (nag * _sigmoid(nag))
    mix = jnp.concatenate([na, rw], axis=-1).astype(BF16)
    h = x_ref[0] + gate_ref[0] * _dot(mix, wo_ref[...])
    ms = jnp.mean(h * h, axis=-1, keepdims=True)
    o_ref[0] = h * lax.rsqrt(ms + RMS_EPS) * fg_ref[...]


OUT_TILE = 512


def _readout_project(yf, yr, bonus, na, qg, x, gate, w_out_bf16, gn_w, gn_b, final_g, seg):
    b, t, d = x.shape
    lat = lambda i, j: (i, j, 0)
    fixed2 = lambda i, j: (0, 0)
    tok = pl.BlockSpec((1, OUT_TILE, WIDTH), lat)
    return pl.pallas_call(
        _out_kernel,
        grid=(b, t // OUT_TILE),
        in_specs=[tok, tok, tok, tok,
                  pl.BlockSpec((1, OUT_TILE, WIDTH), lambda i, j: (i, j, 1)),
                  pl.BlockSpec((1, OUT_TILE, WIDTH), lambda i, j: (i, j, 2)),
                  pl.BlockSpec((1, OUT_TILE, d), lat),
                  pl.BlockSpec((1, 1, d), lambda i, j: (i, 0, 0)),
                  pl.BlockSpec((2 * WIDTH, d), fixed2),
                  pl.BlockSpec((1, WIDTH), fixed2),
                  pl.BlockSpec((1, WIDTH), fixed2),
                  pl.BlockSpec((1, d), fixed2),
                  pl.BlockSpec((WIDTH, WIDTH), fixed2)],
        out_specs=pl.BlockSpec((1, OUT_TILE, d), lat),
        out_shape=jax.ShapeDtypeStruct((b, t, d), F32),
        compiler_params=_params(("parallel", "arbitrary")),
        name="readout_project",
    )(yf, yr, bonus, na, qg, qg, x, gate, w_out_bf16, gn_w, gn_b, final_g, seg)


def _chunk_matrices():
    idx = np.arange(TILE)
    same = (idx[:, None] // CHUNK) == (idx[None, :] // CHUNK)
    tri_f = same & (idx[:, None] >= idx[None, :])
    tri_r = same & (idx[:, None] <= idx[None, :])
    tri = jnp.asarray(np.stack([tri_f, tri_r]).astype(np.float32), dtype=BF16)
    hid = np.arange(WIDTH) // HEAD_DIM
    seg = jnp.asarray((hid[:, None] == hid[None, :]).astype(np.float32), dtype=BF16)
    return tri, seg


def _lora_cat(w2):
    z = jnp.zeros_like(w2[0])
    return jnp.stack([jnp.concatenate([w2[0], z], axis=0), jnp.concatenate([z, w2[1]], axis=0)]).astype(BF16)


def kernel(x, c, ctx, c_ctx, w_mod, b_mod, norm_g, w_in, conv_w, decay_w0, decay_w2, aaa_a0, aaa_a2,
           k_k, k_a, r_k, gn_w, gn_b, na_rpb, w_out, final_g):
    depth = w_mod.shape[0]
    assert depth == 1, "single-layer block: the context stream is never updated"
    b, t, d = x.shape
    l = ctx.shape[1]
    assert l == TILE and t % OUT_TILE == 0 and t // GRID_W >= NA_KH and w_in.shape[2] == D_IN
    n_ctx_tiles = l // TILE

    rows = -(-(b + 1) // 8) * 8
    cvecs = jnp.zeros((rows, d), F32).at[:b].set(c).at[b].set(c_ctx)
    mod = _modulation(cvecs, w_mod[0], b_mod[0])
    shift, scale, gate = mod[:, :d], mod[:, d:2 * d], mod[:, 2 * d:]

    tri, seg = _chunk_matrices()
    kv, qg, *prep = _project_and_prepare(
        x, ctx, shift[:b].reshape(b, 1, d), scale[:b].reshape(b, 1, d), shift[b:b + 1], scale[b:b + 1],
        norm_g[0].reshape(1, d), w_in[0].astype(BF16), conv_w[0], decay_w0[0], _lora_cat(decay_w2[0]),
        aaa_a0[0], _lora_cat(aaa_a2[0]), k_k[0].reshape(1, WIDTH), k_a[0].reshape(1, WIDTH),
        r_k[0].reshape(1, WIDTH), seg, tri)
    yf, yr = _rwkv_scan(prep, l // CHUNK)
    na = _neighborhood_attention(qg, kv, _na_bias_table(na_rpb[0]), l)
    return _readout_project(yf, yr, prep[-1], na, qg, x, gate[:b].reshape(b, 1, d), w_out[0].astype(BF16),
                            gn_w[0].reshape(1, WIDTH), gn_b[0].reshape(1, WIDTH), final_g.reshape(1, d), seg)
```

```python
import functools

import numpy as np
import jax
import jax.numpy as jnp
from jax import lax
from jax.experimental import pallas as pl
from jax.experimental.pallas import tpu as pltpu

F32 = jnp.float32
BF16 = jnp.bfloat16

HEAD_DIM = 64
LANES = 128
NA_HEADS = 8
RW_HEADS = 8
WIDTH = NA_HEADS * HEAD_DIM
N_PAIRS = WIDTH // LANES
GRID_W = 64
NA_KH = 8
NA_KW = 16
LORA = 64
CHUNK = 64
TILE = 256
RMS_EPS = 1e-6
GN_EPS = 64e-5
NEG_BIG = -1e30
LOG2E = 1.4426950408889634
VMEM_LIMIT = 56 * 1024 * 1024

O_NA_K = 0
O_NA_V = WIDTH
O_RW_K = 2 * WIDTH
CONV_W = 3 * WIDTH + 4 * LORA
O_NA_Q = O_RW_K + CONV_W
D_IN = O_NA_Q + 3 * WIDTH


def _dot(a, b):
    return jnp.dot(a, b, preferred_element_type=F32)


def _dot_nt(a, b):
    return lax.dot_general(a, b, (((1,), (1,)), ((), ())), preferred_element_type=F32)


def _dot_tn(a, b):
    return lax.dot_general(a, b, (((0,), (0,)), ((), ())), preferred_element_type=F32)


def _split_dot(m, x, terms):
    acc = None
    rem = x
    for _ in range(terms):
        piece = rem.astype(BF16)
        part = _dot(m, piece)
        acc = part if acc is None else acc + part
        rem = rem - piece.astype(F32)
    return acc


def _split_dot_right(x, m, terms):
    acc = None
    rem = x
    for _ in range(terms):
        piece = rem.astype(BF16)
        part = jnp.concatenate([_dot(piece[:, j:j + LANES], m) for j in range(0, x.shape[1], LANES)], axis=1)
        acc = part if acc is None else acc + part
        rem = rem - piece.astype(F32)
    return acc


def _sigmoid(z):
    return 1.0 / (1.0 + jnp.exp2(z * (-LOG2E)))


def _params(sem):
    return pltpu.CompilerParams(dimension_semantics=sem, vmem_limit_bytes=VMEM_LIMIT)


def _mod_kernel(c_ref, w_ref, b_ref, o_ref):
    cv = c_ref[...]
    o_ref[...] = _dot(cv * _sigmoid(cv), w_ref[...]) + b_ref[...]


def _modulation(cvecs, w_mod, b_mod):
    rows, d = cvecs.shape
    n_out = w_mod.shape[1]
    blk = d
    return pl.pallas_call(
        _mod_kernel,
        grid=(n_out // blk,),
        in_specs=[pl.BlockSpec((rows, d), lambda j: (0, 0)),
                  pl.BlockSpec((d, blk), lambda j: (0, j)),
                  pl.BlockSpec((1, blk), lambda j: (0, j))],
        out_specs=pl.BlockSpec((rows, blk), lambda j: (0, j)),
        out_shape=jax.ShapeDtypeStruct((rows, n_out), F32),
        compiler_params=_params(("arbitrary",)),
        name="modulation",
    )(cvecs, w_mod, b_mod.reshape(1, n_out))


_PROJ_COLS = 256


def _proj_prep_kernel(n_ctx_tiles, x_ref, xprev_ref, xnext_ref, ctx_ref, sh_ref, sc_ref, shc_ref, scc_ref, g_ref,
                      w_ref, cw_ref, w0_ref, w2_ref, a0_ref, a2_ref, kk_w_ref, ka_ref, rk_ref, seg_ref, tri_ref,
                      kv_ref, qg_ref, at0, kt0, bt0, rt0, ec0, at1, kt1, bt1, rt1, ec1, v_ref,
                      bonus_ref, cv_scr, halo_scr):
    tile = pl.program_id(1)
    n_tiles = pl.num_programs(1)
    is_ctx = tile < n_ctx_tiles
    gain = g_ref[...]

    def normed(xin, sh, sc):
        ms = jnp.mean(xin * xin, axis=-1, keepdims=True)
        return ((xin * lax.rsqrt(ms + RMS_EPS) * gain) * (1.0 + sc) + sh).astype(BF16)

    xb = normed(jnp.where(is_ctx, ctx_ref[0], x_ref[0]), jnp.where(is_ctx, shc_ref[...], sh_ref[0]),
                jnp.where(is_ctx, scc_ref[...], sc_ref[0]))

    def project(store, col0, j, scaled):
        res = _dot(xb, w_ref[:, col0 + j:col0 + j + _PROJ_COLS])
        if scaled:
            res = res * (HEAD_DIM ** -0.5 * LOG2E)
        store(slice(j, j + _PROJ_COLS), res.astype(BF16))

    def chunks(store, col0, width, scale_cols=0):
        return [functools.partial(project, store, col0, j, j < scale_cols) for j in range(0, width, _PROJ_COLS)]

    def to_kv(cols, val):
        kv_ref[0, :, cols] = val

    def to_cv(cols, val):
        cv_scr[:, cols] = val

    def to_qg(cols, val):
        qg_ref[0, :, cols] = val

    conv_chunks = chunks(to_cv, O_RW_K, CONV_W)
    pending = chunks(to_kv, O_NA_K, 2 * WIDTH) + chunks(to_qg, O_NA_Q, 3 * WIDTH, scale_cols=WIDTH)

    def more(queue=pending, n=1):
        for _ in range(n):
            if queue:
                queue.pop(0)()

    hb = normed(jnp.concatenate([xprev_ref[0], xnext_ref[0]], axis=0), sh_ref[0], sc_ref[0])
    halo_scr[...] = _dot(hb, w_ref[:, O_RW_K:O_RW_K + CONV_W]).astype(BF16).astype(F32)

    tm = cv_scr.shape[0]
    first = jnp.logical_or(tile == 0, tile == n_ctx_tiles)
    last = jnp.logical_or(tile == n_ctx_tiles - 1, tile == n_tiles - 1)
    row8 = lax.broadcasted_iota(jnp.int32, (8, 1), 0)

    def conv(lo_col, hi_col):
        cols = slice(lo_col, hi_col)
        p = cv_scr[:, cols].astype(F32)
        w_prev, w_mid, w_next = cw_ref[0:1, cols], cw_ref[1:2, cols], cw_ref[2:3, cols]
        u = pltpu.roll(p, 1, axis=0) * w_prev + p * w_mid + pltpu.roll(p, tm - 1, axis=0) * w_next
        prev_row = jnp.where(first, 0.0, halo_scr[7:8, cols])
        next_row = jnp.where(last, 0.0, halo_scr[8:9, cols])
        top = u[0:8] + jnp.where(row8 == 0, (prev_row - p[tm - 1:tm]) * w_prev, 0.0)
        bot = u[tm - 8:] + jnp.where(row8 == 7, (next_row - p[0:1]) * w_next, 0.0)
        return jnp.concatenate([top, u[8:tm - 8], bot], axis=0)

    per_group = WIDTH // _PROJ_COLS
    more(conv_chunks, per_group)
    k = conv(0, WIDTH)
    more(conv_chunks, per_group)
    v = conv(WIDTH, 2 * WIDTH)
    more(conv_chunks, 4 * LORA // _PROJ_COLS)
    lora_in = conv(2 * WIDTH, 2 * WIDTH + 4 * LORA)
    wd = jnp.tanh(lora_in[:, :2 * LORA]).astype(BF16)
    ad = lora_in[:, 2 * LORA:].astype(BF16)
    more(conv_chunks, per_group)
    assert not conv_chunks
    r = conv(2 * WIDTH + 4 * LORA, CONV_W)
    seg = seg_ref[...]

    kk = k * kk_w_ref[...]
    kk = kk * lax.rsqrt(jnp.maximum(_split_dot_right(kk * kk, seg, 2), 1e-24))
    more()
    v_ref[0] = v.astype(BF16)
    k_ka = k * ka_ref[...]
    k_rest = k - k_ka
    r_rk = r * rk_ref[...]
    lw_scale = np.float32(-np.exp(-0.5) * LOG2E)

    kd_sum = None
    outs = ((at0, kt0, bt0, rt0, ec0), (at1, kt1, bt1, rt1, ec1))
    for d, (at, kt, bt, rt, ec) in enumerate(outs):
        lw = lw_scale * _sigmoid(w0_ref[d:d + 1, :] + _dot(wd, w2_ref[d]))
        more()
        a = _sigmoid(a0_ref[d:d + 1, :] + _dot(ad, a2_ref[d]))
        kd = k_rest + k_ka * a
        bb = kk * a
        kd_sum = kd if kd_sum is None else kd_sum + kd
        more()
        cs = _split_dot(tri_ref[d], lw, 2)
        e_inv = jnp.exp2(-cs)
        at[0] = (kk * jnp.exp2(cs - lw)).astype(BF16)
        more()
        kt[0] = (kd * e_inv).astype(BF16)
        bt[0] = (bb * e_inv).astype(BF16)
        more()
        rt[0] = (r * jnp.exp2(cs)).astype(BF16)
        for c in range(tm // CHUNK):
            end = c * CHUNK + (CHUNK - 1 if d == 0 else 0)
            ec[0, c] = jnp.exp2(cs[end:end + 1, :])
    bonus_ref[0] = (_split_dot_right(r_rk * kd_sum, seg, 2) * v).astype(bonus_ref.dtype)
    while pending:
        more()


def _project_and_prepare(x, ctx, shift_m, scale_m, shift_c, scale_c, norm_g, w_in_bf16,
                         conv_w, w0, w2cat, a0, a2cat, k_k, k_a, r_k, seg, tri):
    b, t, d = x.shape
    l = ctx.shape[1]
    n_ctx_tiles = l // TILE
    n_tiles = n_ctx_tiles + t // TILE
    t_tot = l + t
    rows8 = TILE // 8
    lat = lambda i, j: (i, jnp.maximum(j - n_ctx_tiles, 0), 0)
    prev = lambda i, j: (i, jnp.maximum((j - n_ctx_tiles) * rows8 - 1, 0), 0)
    nxt = lambda i, j: (i, jnp.clip((j - n_ctx_tiles + 1) * rows8, 0, t // 8 - 1), 0)
    cidx = lambda i, j: (i, jnp.minimum(j, n_ctx_tiles - 1), 0)
    per_b = lambda i, j: (i, 0, 0)
    fixed2 = lambda i, j: (0, 0)
    fixed3 = lambda i, j: (0, 0, 0)
    out_idx = lambda i, j: (i, j, 0)
    tok = pl.BlockSpec((1, TILE, WIDTH), out_idx)
    ecs = pl.BlockSpec((1, TILE // CHUNK, 1, WIDTH), lambda i, j: (i, j, 0, 0))
    tok_shape = jax.ShapeDtypeStruct((b, t_tot, WIDTH), BF16)
    ec_shape = jax.ShapeDtypeStruct((b, t_tot // CHUNK, 1, WIDTH), F32)
    per_dir_specs = [tok] * 4 + [ecs]
    per_dir_shapes = [tok_shape] * 4 + [ec_shape]
    return pl.pallas_call(
        functools.partial(_proj_prep_kernel, n_ctx_tiles),
        grid=(b, n_tiles),
        in_specs=[pl.BlockSpec((1, TILE, d), lat),
                  pl.BlockSpec((1, 8, d), prev),
                  pl.BlockSpec((1, 8, d), nxt),
                  pl.BlockSpec((1, TILE, d), cidx),
                  pl.BlockSpec((1, 1, d), per_b),
                  pl.BlockSpec((1, 1, d), per_b),
                  pl.BlockSpec((1, d), fixed2),
                  pl.BlockSpec((1, d), fixed2),
                  pl.BlockSpec((1, d), fixed2),
                  pl.BlockSpec((d, D_IN), fixed2),
                  pl.BlockSpec((3, CONV_W), fixed2),
                  pl.BlockSpec((2, WIDTH), fixed2),
                  pl.BlockSpec((2, 2 * LORA, WIDTH), fixed3),
                  pl.BlockSpec((2, WIDTH), fixed2),
                  pl.BlockSpec((2, 2 * LORA, WIDTH), fixed3),
                  pl.BlockSpec((1, WIDTH), fixed2),
                  pl.BlockSpec((1, WIDTH), fixed2),
                  pl.BlockSpec((1, WIDTH), fixed2),
                  pl.BlockSpec((LANES, LANES), fixed2),
                  pl.BlockSpec((2, TILE, TILE), fixed3)],
        out_specs=[pl.BlockSpec((1, TILE, 2 * WIDTH), out_idx),
                   pl.BlockSpec((1, TILE, 3 * WIDTH), lat)] + per_dir_specs * 2
                  + [tok, pl.BlockSpec((1, TILE, WIDTH), lat)],
        out_shape=[jax.ShapeDtypeStruct((b, t_tot, 2 * WIDTH), BF16),
                   jax.ShapeDtypeStruct((b, t, 3 * WIDTH), BF16)] + per_dir_shapes * 2
                  + [tok_shape, jax.ShapeDtypeStruct((b, t, WIDTH), BF16)],
        scratch_shapes=[pltpu.VMEM((TILE, CONV_W), BF16), pltpu.VMEM((16, CONV_W), F32)],
        compiler_params=_params(("parallel", "arbitrary")),
        name="project_and_prepare",
    )(x, x, x, ctx, shift_m, scale_m, shift_c, scale_c, norm_g, w_in_bf16,
      conv_w, w0, w2cat, a0, a2cat, k_k, k_a, r_k, seg, tri)


_INV_SQUARINGS = 5


def _stack(xp, lo):
    zero = jnp.zeros_like(xp)
    return jnp.concatenate([jnp.where(lo, xp, zero), jnp.where(lo, zero, xp)], axis=0)


def _chunk_operators(units, masks):
    lo, strict_f, incl_f, strict_r, incl_r, eye_w, eye_sq, same_head = masks
    cat = jnp.concatenate
    c = CHUNK
    b16 = lambda xs: [x.astype(BF16) for x in xs]
    stack = lambda xs: [_stack(x, lo) for x in xs]
    strict = [strict_r if u["rev"] else strict_f for u in units]
    incl = [incl_r if u["rev"] else incl_f for u in units]
    a_t, k_t, b_t, r_t, v = ([u[n] for u in units] for n in ("a_t", "k_t", "b_t", "r_t", "v"))
    ast, kst, bst, vst = stack(a_t), stack(k_t), stack(b_t), stack(v)
    scores = [_dot_nt(cat([a, r], axis=0), cat([ks, bs], axis=0)) for a, r, ks, bs in zip(a_t, r_t, kst, bst)]
    ak = b16([jnp.where(mk, sc[:c, :LANES], 0.0) for mk, sc in zip(strict, scores)])
    nm = [jnp.where(mk, -sc[:c, LANES:], 0.0) for mk, sc in zip(strict, scores)]
    rkb = b16([jnp.where(mk, sc[c:, :], 0.0) for mk, sc in zip(incl, scores)])
    w1st = stack(b16([_dot(x, vs) for x, vs in zip(ak, vst)]))
    nmb = b16(nm)
    pw = [_dot(x, xs) for x, xs in zip(nmb, stack(nmb))]
    tinv = [eye_w + x for x in nm]
    for _ in range(_INV_SQUARINGS - 1):
        pwb = b16(pw)
        prod = [_dot(cat([pb, t.astype(BF16)], axis=0), ps) for pb, t, ps in zip(pwb, tinv, stack(pwb))]
        pw = [x[:c] for x in prod]
        tinv = [t + x[c:] for t, x in zip(tinv, prod)]
    tinv = [t + _dot(t.astype(BF16), ps) for t, ps in zip(tinv, stack(b16(pw)))]
    gu = b16([_dot(t.astype(BF16), cat([x, w], axis=1)) for t, x, w in zip(tinv, ast, w1st)])
    g = [x[:, :LANES] for x in gu]
    u0 = [x[:, LANES:] for x in gu]
    q = b16([r.astype(F32) - _dot(rb[:, LANES:], gs) for r, rb, gs in zip(r_t, rkb, stack(g))])
    y0 = [_dot(rb, cat([vs, -us], axis=0)) for rb, vs, us in zip(rkb, vst, stack(u0))]
    k_h = b16([x.astype(F32) * u["e_c"] for x, u in zip(k_t, units)])
    b_h = b16([x.astype(F32) * u["e_c"] for x, u in zip(b_t, units)])
    m = b16([jnp.where(same_head, jnp.where(eye_sq, u["e_c"], 0.0) - _dot_tn(bh, gg), 0.0)
             for u, bh, gg in zip(units, b_h, g)])
    dd = [jnp.where(same_head, _dot_tn(cat([kh, bh], axis=0), cat([vv, -uu], axis=0)), 0.0)
          for kh, bh, vv, uu in zip(k_h, b_h, v, u0)]
    return [(cat([qq, mm], axis=0), yy, d2) for qq, mm, yy, d2 in zip(q, m, y0, dd)]


SCAN_CHUNKS = 4


def _scan_kernel(at0, kt0, bt0, rt0, ec0, v0, at1, kt1, bt1, rt1, ec1, v1, y0_ref, y1_ref, s_ref):
    @pl.when(pl.program_id(1) == 0)
    def _():
        s_ref[...] = jnp.zeros_like(s_ref)

    c = CHUNK
    lane_w = lax.broadcasted_iota(jnp.int32, (c, LANES), 1)
    t_w = lax.broadcasted_iota(jnp.int32, (c, LANES), 0)
    s_w = lane_w & (c - 1)
    t_w2 = lax.broadcasted_iota(jnp.int32, (c, 2 * LANES), 0)
    s_w2 = lax.broadcasted_iota(jnp.int32, (c, 2 * LANES), 1) & (c - 1)
    ri = lax.broadcasted_iota(jnp.int32, (LANES, LANES), 0)
    ci = lax.broadcasted_iota(jnp.int32, (LANES, LANES), 1)
    masks = (lane_w < HEAD_DIM,
             t_w > s_w, t_w2 >= s_w2, t_w < s_w, t_w2 <= s_w2,
             jnp.where(t_w == s_w, 1.0, 0.0), ri == ci, (ri // HEAD_DIM) == (ci // HEAD_DIM))
    dirs = ((at0, kt0, bt0, rt0, ec0, v0, y0_ref), (at1, kt1, bt1, rt1, ec1, v1, y1_ref))
    units = []
    where = []
    for j in range(SCAN_CHUNKS):
        for d, (at, kt, bt, rt, ec, vv, y_ref) in enumerate(dirs):
            pos = j if d == 0 else SCAN_CHUNKS - 1 - j
            rows = slice(pos * c, (pos + 1) * c)
            for pr in range(N_PAIRS):
                sl = slice(pr * LANES, (pr + 1) * LANES)
                units.append(dict(rev=d == 1, a_t=at[0, rows, sl], k_t=kt[0, rows, sl], b_t=bt[0, rows, sl],
                                  r_t=rt[0, rows, sl], v=vv[0, rows, sl], e_c=ec[0, pos, :, sl]))
                where.append((d, pr, y_ref, rows, sl))
    ops = _chunk_operators(units, masks)
    n_seq = 2 * N_PAIRS
    state = [s_ref[d, pr] for (d, pr, _, _, _) in where[:n_seq]]
    for j in range(SCAN_CHUNKS):
        step_ops = ops[j * n_seq:(j + 1) * n_seq]
        prods = [_dot(qm, s.astype(BF16)) for (qm, _, _), s in zip(step_ops, state)]
        for x, (_, y0, _), (_, _, y_ref, rows, sl) in zip(prods, step_ops, where[j * n_seq:(j + 1) * n_seq]):
            y_ref[0, rows, sl] = (x[:c] + y0).astype(y_ref.dtype)
        state = [x[c:] + dd for x, (_, _, dd) in zip(prods, step_ops)]
    for s, (d, pr, _, _, _) in zip(state, where[:n_seq]):
        s_ref[d, pr] = s


def _rwkv_scan(prep, n_ctx_chunks):
    (at0, kt0, bt0, rt0, ec0, at1, kt1, bt1, rt1, ec1, v, _) = prep
    b, t_tot, _ = v.shape
    assert t_tot % (SCAN_CHUNKS * CHUNK) == 0 and n_ctx_chunks % SCAN_CHUNKS == 0
    n_blocks = t_tot // (SCAN_CHUNKS * CHUNK)
    n_ctx_blocks = n_ctx_chunks // SCAN_CHUNKS

    def rev_block(g):
        return jnp.where(g < n_ctx_blocks, n_ctx_blocks - 1 - g, n_blocks - 1 + n_ctx_blocks - g)

    fwd = lambda i, g: (i, g, 0)
    rev = lambda i, g: (i, rev_block(g), 0)
    fwd4 = lambda i, g: (i, g, 0, 0)
    rev4 = lambda i, g: (i, rev_block(g), 0, 0)
    tokf = pl.BlockSpec((1, SCAN_CHUNKS * CHUNK, WIDTH), fwd)
    tokr = pl.BlockSpec((1, SCAN_CHUNKS * CHUNK, WIDTH), rev)
    ecf = pl.BlockSpec((1, SCAN_CHUNKS, 1, WIDTH), fwd4)
    ecr = pl.BlockSpec((1, SCAN_CHUNKS, 1, WIDTH), rev4)
    blk_rows = SCAN_CHUNKS * CHUNK
    y_fwd = pl.BlockSpec((1, blk_rows, WIDTH), lambda i, g: (i, jnp.maximum(g - n_ctx_blocks, 0), 0))
    y_rev = pl.BlockSpec((1, blk_rows, WIDTH),
                         lambda i, g: (i, rev_block(jnp.maximum(g, n_ctx_blocks)) - n_ctx_blocks, 0))
    y_shape = jax.ShapeDtypeStruct((b, t_tot - n_ctx_chunks * CHUNK, WIDTH), BF16)
    return pl.pallas_call(
        _scan_kernel,
        grid=(b, n_blocks),
        in_specs=[tokf] * 4 + [ecf, tokf] + [tokr] * 4 + [ecr, tokr],
        out_specs=[y_fwd, y_rev],
        out_shape=[y_shape, y_shape],
        scratch_shapes=[pltpu.VMEM((2, N_PAIRS, 2 * HEAD_DIM, 2 * HEAD_DIM), F32)],
        compiler_params=_params(("parallel", "arbitrary")),
        name="rwkv_scan",
    )(at0, kt0, bt0, rt0, ec0, v, at1, kt1, bt1, rt1, ec1, v)


NA_ROWS = 32


def _na_kernel(ctx_len, n_rows, q_ref, k_ref, v_ref, bias_ref, o_ref):
    step = pl.program_id(2)
    win = NA_KH * GRID_W
    kc = k_ref[0, 0:ctx_len, :]
    vc = v_ref[0, 0:ctx_len, :]
    lo = lax.broadcasted_iota(jnp.int32, (GRID_W, LANES), 1) < HEAD_DIM
    rows = range(NA_ROWS)
    r0 = [jnp.clip(step * NA_ROWS + j - NA_KH // 2, 0, n_rows - NA_KH) for j in rows]
    delta = [step * NA_ROWS + j - r for j, r in zip(rows, r0)]
    start = [pl.multiple_of(ctx_len + r * GRID_W, GRID_W) for r in r0]
    qst = [_stack(q_ref[0, j * GRID_W:(j + 1) * GRID_W, :], lo) for j in rows]
    s_loc = [_dot_nt(q, k_ref[0, pl.ds(s, win), :]) + bias_ref[dl, 0] for q, s, dl in zip(qst, start, delta)]
    s_ctx = [_dot_nt(q, kc) for q in qst]
    mx = [jnp.maximum(jnp.max(a, axis=-1, keepdims=True), jnp.max(c, axis=-1, keepdims=True))
          for a, c in zip(s_loc, s_ctx)]
    p_loc = [jnp.exp2(a - m).astype(BF16) for a, m in zip(s_loc, mx)]
    p_ctx = [jnp.exp2(c - m).astype(BF16) for c, m in zip(s_ctx, mx)]
    ones_loc = jnp.ones((win, LANES), BF16)
    ones_ctx = jnp.ones((ctx_len, LANES), BF16)
    vc1 = jnp.concatenate([vc, ones_ctx], axis=1)
    od = [_dot(a, jnp.concatenate([v_ref[0, pl.ds(s, win), :], ones_loc], axis=1)) + _dot(c, vc1)
          for a, c, s in zip(p_loc, p_ctx, start)]
    for j, x in zip(rows, od):
        oj = x[:, :LANES] / x[:, LANES:]
        o_ref[0, j * GRID_W:(j + 1) * GRID_W, :] = jnp.where(lo, oj[:GRID_W], oj[GRID_W:]).astype(o_ref.dtype)


def _neighborhood_attention(qg, kv, bias, ctx_len):
    b, t_tot, _ = kv.shape
    t = t_tot - ctx_len
    n_rows = t // GRID_W
    assert n_rows % NA_ROWS == 0
    blk_rows = NA_ROWS * GRID_W
    return pl.pallas_call(
        functools.partial(_na_kernel, ctx_len, n_rows),
        grid=(b, N_PAIRS, n_rows // NA_ROWS),
        in_specs=[pl.BlockSpec((1, blk_rows, LANES), lambda bi, pr, i: (bi, i, pr)),
                  pl.BlockSpec((1, t_tot, LANES), lambda bi, pr, i: (bi, 0, pr)),
                  pl.BlockSpec((1, t_tot, LANES), lambda bi, pr, i: (bi, 0, N_PAIRS + pr)),
                  pl.BlockSpec((NA_KH, 1, 2 * GRID_W, NA_KH * GRID_W), lambda bi, pr, i: (0, pr, 0, 0))],
        out_specs=pl.BlockSpec((1, blk_rows, LANES), lambda bi, pr, i: (bi, i, pr)),
        out_shape=jax.ShapeDtypeStruct((b, t, WIDTH), BF16),
        compiler_params=_params(("parallel", "parallel", "arbitrary")),
        name="neighborhood_attention",
    )(qg, kv, kv, bias)


def _na_bias_table(na_rpb):
    col = np.arange(GRID_W)
    c0 = np.clip(col - NA_KW // 2, 0, GRID_W - NA_KW)
    valid = (col[None, :] >= c0[:, None]) & (col[None, :] < c0[:, None] + NA_KW)
    col_off = np.clip(col[None, :] - col[:, None] + NA_KW - 1, 0, 2 * NA_KW - 2)
    row_off = np.arange(NA_KH)[None, :] - np.arange(NA_KH)[:, None] + NA_KH - 1
    n_col = 2 * NA_KW - 1
    sel_r = (row_off[..., None] == np.arange(2 * NA_KH - 1)).astype(np.float32)
    sel_c = (np.where(valid, col_off, n_col)[..., None] == np.arange(n_col + 1)).astype(np.float32)
    src = jnp.concatenate([na_rpb.astype(F32) * LOG2E, jnp.full(na_rpb.shape[:2] + (1,), NEG_BIG, F32)], axis=-1)
    tbl = jnp.einsum('dri,hij,qcj->dhqrc', sel_r, src, sel_c, precision=lax.Precision.HIGHEST)
    return tbl.reshape(NA_KH, N_PAIRS, 2 * GRID_W, NA_KH * GRID_W)


def _out_kernel(yf_ref, yr_ref, bonus_ref, na_ref, nag_ref, rwg_ref, x_ref, gate_ref, wo_ref,
                gnw_ref, gnb_ref, fg_ref, seg_ref, o_ref):
    seg = seg_ref[...]
    y = yf_ref[0].astype(F32) + yr_ref[0].astype(F32)
    inv_n = 1.0 / HEAD_DIM
    mu = _split_dot_right(y, seg, 2) * inv_n
    dev = y - mu
    var = _split_dot_right(dev * dev, seg, 2) * inv_n
    yn = dev * lax.rsqrt(var + GN_EPS) * gnw_ref[...] + gnb_ref[...]
    rwg = rwg_ref[0].astype(F32)
    nag = nag_ref[0].astype(F32)
    rw = (yn + bonus_ref[0].astype(F32)) * (rwg * _sigmoid(rwg))
    na = na_ref[0].astype(F32) * (nag * _sigmoid(nag))
    mix = jnp.concatenate([na, rw], axis=-1).astype(BF16)
    h = x_ref[0] + gate_ref[0] * _dot(mix, wo_ref[...])
    ms = jnp.mean(h * h, axis=-1, keepdims=True)
    o_ref[0] = h * lax.rsqrt(ms + RMS_EPS) * fg_ref[...]


OUT_TILE = 512


def _readout_project(yf, yr, bonus, na, qg, x, gate, w_out_bf16, gn_w, gn_b, final_g, seg):
    b, t, d = x.shape
    lat = lambda i, j: (i, j, 0)
    fixed2 = lambda i, j: (0, 0)
    tok = pl.BlockSpec((1, OUT_TILE, WIDTH), lat)
    return pl.pallas_call(
        _out_kernel,
        grid=(b, t // OUT_TILE),
        in_specs=[tok, tok, tok, tok,
                  pl.BlockSpec((1, OUT_TILE, WIDTH), lambda i, j: (i, j, 1)),
                  pl.BlockSpec((1, OUT_TILE, WIDTH), lambda i, j: (i, j, 2)),
                  pl.BlockSpec((1, OUT_TILE, d), lat),
                  pl.BlockSpec((1, 1, d), lambda i, j: (i, 0, 0)),
                  pl.BlockSpec((2 * WIDTH, d), fixed2),
                  pl.BlockSpec((1, WIDTH), fixed2),
                  pl.BlockSpec((1, WIDTH), fixed2),
                  pl.BlockSpec((1, d), fixed2),
                  pl.BlockSpec((LANES, LANES), fixed2)],
        out_specs=pl.BlockSpec((1, OUT_TILE, d), lat),
        out_shape=jax.ShapeDtypeStruct((b, t, d), F32),
        compiler_params=_params(("parallel", "arbitrary")),
        name="readout_project",
    )(yf, yr, bonus, na, qg, qg, x, gate, w_out_bf16, gn_w, gn_b, final_g, seg)


def _chunk_matrices():
    idx = np.arange(TILE)
    same = (idx[:, None] // CHUNK) == (idx[None, :] // CHUNK)
    tri_f = same & (idx[:, None] >= idx[None, :])
    tri_r = same & (idx[:, None] <= idx[None, :])
    tri = jnp.asarray(np.stack([tri_f, tri_r]).astype(np.float32), dtype=BF16)
    hid = np.arange(LANES) // HEAD_DIM
    seg = jnp.asarray((hid[:, None] == hid[None, :]).astype(np.float32), dtype=BF16)
    return tri, seg


def _lora_cat(w2):
    z = jnp.zeros_like(w2[0])
    return jnp.stack([jnp.concatenate([w2[0], z], axis=0), jnp.concatenate([z, w2[1]], axis=0)]).astype(BF16)


def kernel(x, c, ctx, c_ctx, w_mod, b_mod, norm_g, w_in, conv_w, decay_w0, decay_w2, aaa_a0, aaa_a2,
           k_k, k_a, r_k, gn_w, gn_b, na_rpb, w_out, final_g):
    depth = w_mod.shape[0]
    assert depth == 1, "single-layer block: the context stream is never updated"
    b, t, d = x.shape
    l = ctx.shape[1]
    assert l == TILE and t % OUT_TILE == 0 and t // GRID_W >= NA_KH and w_in.shape[2] == D_IN
    n_ctx_tiles = l // TILE

    rows = -(-(b + 1) // 8) * 8
    cvecs = jnp.zeros((rows, d), F32).at[:b].set(c).at[b].set(c_ctx)
    mod = _modulation(cvecs, w_mod[0], b_mod[0])
    shift, scale, gate = mod[:, :d], mod[:, d:2 * d], mod[:, 2 * d:]

    tri, seg = _chunk_matrices()
    kv, qg, *prep = _project_and_prepare(
        x, ctx, shift[:b].reshape(b, 1, d), scale[:b].reshape(b, 1, d), shift[b:b + 1], scale[b:b + 1],
        norm_g[0].reshape(1, d), w_in[0].astype(BF16), conv_w[0], decay_w0[0], _lora_cat(decay_w2[0]),
        aaa_a0[0], _lora_cat(aaa_a2[0]), k_k[0].reshape(1, WIDTH), k_a[0].reshape(1, WIDTH),
        r_k[0].reshape(1, WIDTH), seg, tri)
    yf, yr = _rwkv_scan(prep, l // CHUNK)
    na = _neighborhood_attention(qg, kv, _na_bias_table(na_rpb[0]), l)
    return _readout_project(yf, yr, prep[-1], na, qg, x, gate[:b].reshape(b, 1, d), w_out[0].astype(BF16),
                            gn_w[0].reshape(1, WIDTH), gn_b[0].reshape(1, WIDTH), final_g.reshape(1, d), seg)
```

```python
import functools

import numpy as np
import jax
import jax.numpy as jnp
from jax import lax
from jax.experimental import pallas as pl
from jax.experimental.pallas import tpu as pltpu

F32 = jnp.float32
BF16 = jnp.bfloat16

HEAD_DIM = 64
LANES = 128
NA_HEADS = 8
RW_HEADS = 8
WIDTH = NA_HEADS * HEAD_DIM
N_PAIRS = WIDTH // LANES
GRID_W = 64
NA_KH = 8
NA_KW = 16
LORA = 64
CHUNK = 64
TILE = 256
RMS_EPS = 1e-6
GN_EPS = 64e-5
NEG_BIG = -1e30
LOG2E = 1.4426950408889634
VMEM_LIMIT = 56 * 1024 * 1024

O_NA_K = 0
O_NA_V = WIDTH
O_RW_K = 2 * WIDTH
CONV_W = 3 * WIDTH + 4 * LORA
O_NA_Q = O_RW_K + CONV_W
D_IN = O_NA_Q + 3 * WIDTH


def _dot(a, b):
    return jnp.dot(a, b, preferred_element_type=F32)


def _dot_nt(a, b):
    return lax.dot_general(a, b, (((1,), (1,)), ((), ())), preferred_element_type=F32)


def _dot_tn(a, b):
    return lax.dot_general(a, b, (((0,), (0,)), ((), ())), preferred_element_type=F32)


def _split_dot(m, x, terms):
    acc = None
    rem = x
    for _ in range(terms):
        piece = rem.astype(BF16)
        part = _dot(m, piece)
        acc = part if acc is None else acc + part
        rem = rem - piece.astype(F32)
    return acc


def _split_dot_right(x, m, terms, pack=False):
    if pack:
        assert terms * LANES <= 256
        pieces = []
        rem = x
        for _ in range(terms):
            pieces.append(rem.astype(BF16))
            rem = rem - pieces[-1].astype(F32)
        m_rep = jnp.concatenate([m] * terms, axis=0)
        return jnp.concatenate(
            [_dot(jnp.concatenate([p[:, j:j + LANES] for p in pieces], axis=1), m_rep)
             for j in range(0, x.shape[1], LANES)], axis=1)
    acc = None
    rem = x
    for _ in range(terms):
        piece = rem.astype(BF16)
        part = jnp.concatenate([_dot(piece[:, j:j + LANES], m) for j in range(0, x.shape[1], LANES)], axis=1)
        acc = part if acc is None else acc + part
        rem = rem - piece.astype(F32)
    return acc


def _sigmoid(z):
    return 1.0 / (1.0 + jnp.exp2(z * (-LOG2E)))


def _params(sem):
    return pltpu.CompilerParams(dimension_semantics=sem, vmem_limit_bytes=VMEM_LIMIT)


def _mod_kernel(c_ref, w_ref, b_ref, o_ref):
    cv = c_ref[...]
    o_ref[...] = _dot(cv * _sigmoid(cv), w_ref[...]) + b_ref[...]


def _modulation(cvecs, w_mod, b_mod):
    rows, d = cvecs.shape
    n_out = w_mod.shape[1]
    blk = d
    return pl.pallas_call(
        _mod_kernel,
        grid=(n_out // blk,),
        in_specs=[pl.BlockSpec((rows, d), lambda j: (0, 0)),
                  pl.BlockSpec((d, blk), lambda j: (0, j)),
                  pl.BlockSpec((1, blk), lambda j: (0, j))],
        out_specs=pl.BlockSpec((rows, blk), lambda j: (0, j)),
        out_shape=jax.ShapeDtypeStruct((rows, n_out), F32),
        compiler_params=_params(("arbitrary",)),
        name="modulation",
    )(cvecs, w_mod, b_mod.reshape(1, n_out))


_PROJ_COLS = 256


def _proj_prep_kernel(n_ctx_tiles, x_ref, xprev_ref, xnext_ref, ctx_ref, sh_ref, sc_ref, shc_ref, scc_ref, g_ref,
                      w_ref, cw_ref, w0_ref, w2_ref, a0_ref, a2_ref, kk_w_ref, ka_ref, rk_ref, seg_ref, tri_ref,
                      kv_ref, qg_ref, at0, kt0, bt0, rt0, ec0, at1, kt1, bt1, rt1, ec1, v_ref,
                      bonus_ref, cv_scr, halo_scr):
    tile = pl.program_id(1)
    n_tiles = pl.num_programs(1)
    is_ctx = tile < n_ctx_tiles
    gain = g_ref[...]

    def normed(xin, sh, sc):
        ms = jnp.mean(xin * xin, axis=-1, keepdims=True)
        return ((xin * lax.rsqrt(ms + RMS_EPS) * gain) * (1.0 + sc) + sh).astype(BF16)

    xb = normed(jnp.where(is_ctx, ctx_ref[0], x_ref[0]), jnp.where(is_ctx, shc_ref[...], sh_ref[0]),
                jnp.where(is_ctx, scc_ref[...], sc_ref[0]))

    def project(store, col0, j, scaled):
        res = _dot(xb, w_ref[:, col0 + j:col0 + j + _PROJ_COLS])
        if scaled:
            res = res * (HEAD_DIM ** -0.5 * LOG2E)
        store(slice(j, j + _PROJ_COLS), res.astype(BF16))

    def chunks(store, col0, width, scale_cols=0):
        return [functools.partial(project, store, col0, j, j < scale_cols) for j in range(0, width, _PROJ_COLS)]

    def to_kv(cols, val):
        kv_ref[0, :, cols] = val

    def to_cv(cols, val):
        cv_scr[:, cols] = val

    def to_qg(cols, val):
        qg_ref[0, :, cols] = val

    conv_chunks = chunks(to_cv, O_RW_K, CONV_W)
    pending = chunks(to_kv, O_NA_K, 2 * WIDTH) + chunks(to_qg, O_NA_Q, 3 * WIDTH, scale_cols=WIDTH)

    def more(queue=pending, n=1):
        for _ in range(n):
            if queue:
                queue.pop(0)()

    hb = normed(jnp.concatenate([xprev_ref[0], xnext_ref[0]], axis=0), sh_ref[0], sc_ref[0])
    halo_scr[...] = _dot(hb, w_ref[:, O_RW_K:O_RW_K + CONV_W]).astype(BF16).astype(F32)

    tm = cv_scr.shape[0]
    first = jnp.logical_or(tile == 0, tile == n_ctx_tiles)
    last = jnp.logical_or(tile == n_ctx_tiles - 1, tile == n_tiles - 1)
    row8 = lax.broadcasted_iota(jnp.int32, (8, 1), 0)

    def conv(lo_col, hi_col):
        cols = slice(lo_col, hi_col)
        p = cv_scr[:, cols].astype(F32)
        w_prev, w_mid, w_next = cw_ref[0:1, cols], cw_ref[1:2, cols], cw_ref[2:3, cols]
        u = pltpu.roll(p, 1, axis=0) * w_prev + p * w_mid + pltpu.roll(p, tm - 1, axis=0) * w_next
        prev_row = jnp.where(first, 0.0, halo_scr[7:8, cols])
        next_row = jnp.where(last, 0.0, halo_scr[8:9, cols])
        top = u[0:8] + jnp.where(row8 == 0, (prev_row - p[tm - 1:tm]) * w_prev, 0.0)
        bot = u[tm - 8:] + jnp.where(row8 == 7, (next_row - p[0:1]) * w_next, 0.0)
        return jnp.concatenate([top, u[8:tm - 8], bot], axis=0)

    per_group = WIDTH // _PROJ_COLS
    more(conv_chunks, per_group)
    k = conv(0, WIDTH)
    more(conv_chunks, per_group)
    v = conv(WIDTH, 2 * WIDTH)
    more(conv_chunks, 4 * LORA // _PROJ_COLS)
    lora_in = conv(2 * WIDTH, 2 * WIDTH + 4 * LORA)
    wd = jnp.tanh(lora_in[:, :2 * LORA]).astype(BF16)
    ad = lora_in[:, 2 * LORA:].astype(BF16)
    more(conv_chunks, per_group)
    assert not conv_chunks
    r = conv(2 * WIDTH + 4 * LORA, CONV_W)
    seg = seg_ref[...]

    kk = k * kk_w_ref[...]
    kk = kk * lax.rsqrt(jnp.maximum(_split_dot_right(kk * kk, seg, 2), 1e-24))
    more()
    v_ref[0] = v.astype(BF16)
    k_ka = k * ka_ref[...]
    k_rest = k - k_ka
    r_rk = r * rk_ref[...]
    lw_scale = np.float32(-np.exp(-0.5) * LOG2E)

    kd_sum = None
    outs = ((at0, kt0, bt0, rt0, ec0), (at1, kt1, bt1, rt1, ec1))
    for d, (at, kt, bt, rt, ec) in enumerate(outs):
        lw = lw_scale * _sigmoid(w0_ref[d:d + 1, :] + _dot(wd, w2_ref[d]))
        more()
        a = _sigmoid(a0_ref[d:d + 1, :] + _dot(ad, a2_ref[d]))
        kd = k_rest + k_ka * a
        bb = kk * a
        kd_sum = kd if kd_sum is None else kd_sum + kd
        more()
        cs = _split_dot(tri_ref[d], lw, 2)
        e_inv = jnp.exp2(-cs)
        at[0] = (kk * jnp.exp2(cs - lw)).astype(BF16)
        more()
        kt[0] = (kd * e_inv).astype(BF16)
        bt[0] = (bb * e_inv).astype(BF16)
        more()
        rt[0] = (r * jnp.exp2(cs)).astype(BF16)
        for c in range(tm // CHUNK):
            end = c * CHUNK + (CHUNK - 1 if d == 0 else 0)
            ec[0, c] = jnp.exp2(cs[end:end + 1, :])
    bonus_ref[0] = (_split_dot_right(r_rk * kd_sum, seg, 2) * v).astype(bonus_ref.dtype)
    while pending:
        more()


def _project_and_prepare(x, ctx, shift_m, scale_m, shift_c, scale_c, norm_g, w_in_bf16,
                         conv_w, w0, w2cat, a0, a2cat, k_k, k_a, r_k, seg, tri):
    b, t, d = x.shape
    l = ctx.shape[1]
    n_ctx_tiles = l // TILE
    n_tiles = n_ctx_tiles + t // TILE
    t_tot = l + t
    rows8 = TILE // 8
    lat = lambda i, j: (i, jnp.maximum(j - n_ctx_tiles, 0), 0)
    prev = lambda i, j: (i, jnp.maximum((j - n_ctx_tiles) * rows8 - 1, 0), 0)
    nxt = lambda i, j: (i, jnp.clip((j - n_ctx_tiles + 1) * rows8, 0, t // 8 - 1), 0)
    cidx = lambda i, j: (i, jnp.minimum(j, n_ctx_tiles - 1), 0)
    per_b = lambda i, j: (i, 0, 0)
    fixed2 = lambda i, j: (0, 0)
    fixed3 = lambda i, j: (0, 0, 0)
    out_idx = lambda i, j: (i, j, 0)
    tok = pl.BlockSpec((1, TILE, WIDTH), out_idx)
    ecs = pl.BlockSpec((1, TILE // CHUNK, 1, WIDTH), lambda i, j: (i, j, 0, 0))
    tok_shape = jax.ShapeDtypeStruct((b, t_tot, WIDTH), BF16)
    ec_shape = jax.ShapeDtypeStruct((b, t_tot // CHUNK, 1, WIDTH), F32)
    per_dir_specs = [tok] * 4 + [ecs]
    per_dir_shapes = [tok_shape] * 4 + [ec_shape]
    return pl.pallas_call(
        functools.partial(_proj_prep_kernel, n_ctx_tiles),
        grid=(b, n_tiles),
        in_specs=[pl.BlockSpec((1, TILE, d), lat),
                  pl.BlockSpec((1, 8, d), prev),
                  pl.BlockSpec((1, 8, d), nxt),
                  pl.BlockSpec((1, TILE, d), cidx),
                  pl.BlockSpec((1, 1, d), per_b),
                  pl.BlockSpec((1, 1, d), per_b),
                  pl.BlockSpec((1, d), fixed2),
                  pl.BlockSpec((1, d), fixed2),
                  pl.BlockSpec((1, d), fixed2),
                  pl.BlockSpec((d, D_IN), fixed2),
                  pl.BlockSpec((3, CONV_W), fixed2),
                  pl.BlockSpec((2, WIDTH), fixed2),
                  pl.BlockSpec((2, 2 * LORA, WIDTH), fixed3),
                  pl.BlockSpec((2, WIDTH), fixed2),
                  pl.BlockSpec((2, 2 * LORA, WIDTH), fixed3),
                  pl.BlockSpec((1, WIDTH), fixed2),
                  pl.BlockSpec((1, WIDTH), fixed2),
                  pl.BlockSpec((1, WIDTH), fixed2),
                  pl.BlockSpec((LANES, LANES), fixed2),
                  pl.BlockSpec((2, TILE, TILE), fixed3)],
        out_specs=[pl.BlockSpec((1, TILE, 2 * WIDTH), out_idx),
                   pl.BlockSpec((1, TILE, 3 * WIDTH), lat)] + per_dir_specs * 2
                  + [tok, pl.BlockSpec((1, TILE, WIDTH), lat)],
        out_shape=[jax.ShapeDtypeStruct((b, t_tot, 2 * WIDTH), BF16),
                   jax.ShapeDtypeStruct((b, t, 3 * WIDTH), BF16)] + per_dir_shapes * 2
                  + [tok_shape, jax.ShapeDtypeStruct((b, t, WIDTH), BF16)],
        scratch_shapes=[pltpu.VMEM((TILE, CONV_W), BF16), pltpu.VMEM((16, CONV_W), F32)],
        compiler_params=_params(("parallel", "arbitrary")),
        name="project_and_prepare",
    )(x, x, x, ctx, shift_m, scale_m, shift_c, scale_c, norm_g, w_in_bf16,
      conv_w, w0, w2cat, a0, a2cat, k_k, k_a, r_k, seg, tri)


_INV_SQUARINGS = 5


def _stack(xp, lo):
    zero = jnp.zeros_like(xp)
    return jnp.concatenate([jnp.where(lo, xp, zero), jnp.where(lo, zero, xp)], axis=0)


def _chunk_operators(units, masks):
    lo, strict_f, incl_f, strict_r, incl_r, eye_w, eye_sq, same_head = masks
    cat = jnp.concatenate
    c = CHUNK
    b16 = lambda xs: [x.astype(BF16) for x in xs]
    stack = lambda xs: [_stack(x, lo) for x in xs]
    strict = [strict_r if u["rev"] else strict_f for u in units]
    incl = [incl_r if u["rev"] else incl_f for u in units]
    a_t, k_t, b_t, r_t, v = ([u[n] for u in units] for n in ("a_t", "k_t", "b_t", "r_t", "v"))
    ast, kst, bst, vst = stack(a_t), stack(k_t), stack(b_t), stack(v)
    scores = [_dot_nt(cat([a, r], axis=0), cat([ks, bs], axis=0)) for a, r, ks, bs in zip(a_t, r_t, kst, bst)]
    ak = b16([jnp.where(mk, sc[:c, :LANES], 0.0) for mk, sc in zip(strict, scores)])
    nm = [jnp.where(mk, -sc[:c, LANES:], 0.0) for mk, sc in zip(strict, scores)]
    rkb = b16([jnp.where(mk, sc[c:, :], 0.0) for mk, sc in zip(incl, scores)])
    w1st = stack(b16([_dot(x, vs) for x, vs in zip(ak, vst)]))
    nmb = b16(nm)
    pw = [_dot(x, xs) for x, xs in zip(nmb, stack(nmb))]
    tinv = [eye_w + x for x in nm]
    for _ in range(_INV_SQUARINGS - 1):
        pwb = b16(pw)
        prod = [_dot(cat([pb, t.astype(BF16)], axis=0), ps) for pb, t, ps in zip(pwb, tinv, stack(pwb))]
        pw = [x[:c] for x in prod]
        tinv = [t + x[c:] for t, x in zip(tinv, prod)]
    tinv = [t + _dot(t.astype(BF16), ps) for t, ps in zip(tinv, stack(b16(pw)))]
    gu = b16([_dot(t.astype(BF16), cat([x, w], axis=1)) for t, x, w in zip(tinv, ast, w1st)])
    g = [x[:, :LANES] for x in gu]
    u0 = [x[:, LANES:] for x in gu]
    q = b16([r.astype(F32) - _dot(rb[:, LANES:], gs) for r, rb, gs in zip(r_t, rkb, stack(g))])
    y0 = [_dot(rb, cat([vs, -us], axis=0)) for rb, vs, us in zip(rkb, vst, stack(u0))]
    k_h = b16([x.astype(F32) * u["e_c"] for x, u in zip(k_t, units)])
    b_h = b16([x.astype(F32) * u["e_c"] for x, u in zip(b_t, units)])
    m = b16([jnp.where(same_head, jnp.where(eye_sq, u["e_c"], 0.0) - _dot_tn(bh, gg), 0.0)
             for u, bh, gg in zip(units, b_h, g)])
    dd = [jnp.where(same_head, _dot_tn(cat([kh, bh], axis=0), cat([vv, -uu], axis=0)), 0.0)
          for kh, bh, vv, uu in zip(k_h, b_h, v, u0)]
    return [(cat([qq, mm], axis=0), yy, d2) for qq, mm, yy, d2 in zip(q, m, y0, dd)]


SCAN_CHUNKS = 4


def _scan_kernel(at0, kt0, bt0, rt0, ec0, v0, at1, kt1, bt1, rt1, ec1, v1, y0_ref, y1_ref, s_ref):
    @pl.when(pl.program_id(1) == 0)
    def _():
        s_ref[...] = jnp.zeros_like(s_ref)

    c = CHUNK
    lane_w = lax.broadcasted_iota(jnp.int32, (c, LANES), 1)
    t_w = lax.broadcasted_iota(jnp.int32, (c, LANES), 0)
    s_w = lane_w & (c - 1)
    t_w2 = lax.broadcasted_iota(jnp.int32, (c, 2 * LANES), 0)
    s_w2 = lax.broadcasted_iota(jnp.int32, (c, 2 * LANES), 1) & (c - 1)
    ri = lax.broadcasted_iota(jnp.int32, (LANES, LANES), 0)
    ci = lax.broadcasted_iota(jnp.int32, (LANES, LANES), 1)
    masks = (lane_w < HEAD_DIM,
             t_w > s_w, t_w2 >= s_w2, t_w < s_w, t_w2 <= s_w2,
             jnp.where(t_w == s_w, 1.0, 0.0), ri == ci, (ri // HEAD_DIM) == (ci // HEAD_DIM))
    dirs = ((at0, kt0, bt0, rt0, ec0, v0, y0_ref), (at1, kt1, bt1, rt1, ec1, v1, y1_ref))
    units = []
    where = []
    for j in range(SCAN_CHUNKS):
        for d, (at, kt, bt, rt, ec, vv, y_ref) in enumerate(dirs):
            pos = j if d == 0 else SCAN_CHUNKS - 1 - j
            rows = slice(pos * c, (pos + 1) * c)
            for pr in range(N_PAIRS):
                sl = slice(pr * LANES, (pr + 1) * LANES)
                units.append(dict(rev=d == 1, a_t=at[0, rows, sl], k_t=kt[0, rows, sl], b_t=bt[0, rows, sl],
                                  r_t=rt[0, rows, sl], v=vv[0, rows, sl], e_c=ec[0, pos, :, sl]))
                where.append((d, pr, y_ref, rows, sl))
    ops = _chunk_operators(units, masks)
    n_seq = 2 * N_PAIRS
    state = [s_ref[d, pr] for (d, pr, _, _, _) in where[:n_seq]]
    for j in range(SCAN_CHUNKS):
        step_ops = ops[j * n_seq:(j + 1) * n_seq]
        prods = [_dot(qm, s.astype(BF16)) for (qm, _, _), s in zip(step_ops, state)]
        for x, (_, y0, _), (_, _, y_ref, rows, sl) in zip(prods, step_ops, where[j * n_seq:(j + 1) * n_seq]):
            y_ref[0, rows, sl] = (x[:c] + y0).astype(y_ref.dtype)
        state = [x[c:] + dd for x, (_, _, dd) in zip(prods, step_ops)]
    for s, (d, pr, _, _, _) in zip(state, where[:n_seq]):
        s_ref[d, pr] = s


def _rwkv_scan(prep, n_ctx_chunks):
    (at0, kt0, bt0, rt0, ec0, at1, kt1, bt1, rt1, ec1, v, _) = prep
    b, t_tot, _ = v.shape
    assert t_tot % (SCAN_CHUNKS * CHUNK) == 0 and n_ctx_chunks % SCAN_CHUNKS == 0
    n_blocks = t_tot // (SCAN_CHUNKS * CHUNK)
    n_ctx_blocks = n_ctx_chunks // SCAN_CHUNKS

    def rev_block(g):
        return jnp.where(g < n_ctx_blocks, n_ctx_blocks - 1 - g, n_blocks - 1 + n_ctx_blocks - g)

    fwd = lambda i, g: (i, g, 0)
    rev = lambda i, g: (i, rev_block(g), 0)
    fwd4 = lambda i, g: (i, g, 0, 0)
    rev4 = lambda i, g: (i, rev_block(g), 0, 0)
    tokf = pl.BlockSpec((1, SCAN_CHUNKS * CHUNK, WIDTH), fwd)
    tokr = pl.BlockSpec((1, SCAN_CHUNKS * CHUNK, WIDTH), rev)
    ecf = pl.BlockSpec((1, SCAN_CHUNKS, 1, WIDTH), fwd4)
    ecr = pl.BlockSpec((1, SCAN_CHUNKS, 1, WIDTH), rev4)
    blk_rows = SCAN_CHUNKS * CHUNK
    y_fwd = pl.BlockSpec((1, blk_rows, WIDTH), lambda i, g: (i, jnp.maximum(g - n_ctx_blocks, 0), 0))
    y_rev = pl.BlockSpec((1, blk_rows, WIDTH),
                         lambda i, g: (i, rev_block(jnp.maximum(g, n_ctx_blocks)) - n_ctx_blocks, 0))
    y_shape = jax.ShapeDtypeStruct((b, t_tot - n_ctx_chunks * CHUNK, WIDTH), BF16)
    return pl.pallas_call(
        _scan_kernel,
        grid=(b, n_blocks),
        in_specs=[tokf] * 4 + [ecf, tokf] + [tokr] * 4 + [ecr, tokr],
        out_specs=[y_fwd, y_rev],
        out_shape=[y_shape, y_shape],
        scratch_shapes=[pltpu.VMEM((2, N_PAIRS, 2 * HEAD_DIM, 2 * HEAD_DIM), F32)],
        compiler_params=_params(("parallel", "arbitrary")),
        name="rwkv_scan",
    )(at0, kt0, bt0, rt0, ec0, v, at1, kt1, bt1, rt1, ec1, v)


NA_ROWS = 32


def _na_kernel(ctx_len, n_rows, q_ref, k_ref, v_ref, bias_ref, o_ref):
    step = pl.program_id(2)
    win = NA_KH * GRID_W
    kc = k_ref[0, 0:ctx_len, :]
    vc = v_ref[0, 0:ctx_len, :]
    lo = lax.broadcasted_iota(jnp.int32, (GRID_W, LANES), 1) < HEAD_DIM
    rows = range(NA_ROWS)
    r0 = [jnp.clip(step * NA_ROWS + j - NA_KH // 2, 0, n_rows - NA_KH) for j in rows]
    delta = [step * NA_ROWS + j - r for j, r in zip(rows, r0)]
    start = [pl.multiple_of(ctx_len + r * GRID_W, GRID_W) for r in r0]
    qst = [_stack(q_ref[0, j * GRID_W:(j + 1) * GRID_W, :], lo) for j in rows]
    s_loc = [_dot_nt(q, k_ref[0, pl.ds(s, win), :]) + bias_ref[dl, 0] for q, s, dl in zip(qst, start, delta)]
    s_ctx = [_dot_nt(q, kc) for q in qst]
    mx = [jnp.maximum(jnp.max(a, axis=-1, keepdims=True), jnp.max(c, axis=-1, keepdims=True))
          for a, c in zip(s_loc, s_ctx)]
    p_loc = [jnp.exp2(a - m).astype(BF16) for a, m in zip(s_loc, mx)]
    p_ctx = [jnp.exp2(c - m).astype(BF16) for c, m in zip(s_ctx, mx)]
    ones_loc = jnp.ones((win, LANES), BF16)
    ones_ctx = jnp.ones((ctx_len, LANES), BF16)
    vc1 = jnp.concatenate([vc, ones_ctx], axis=1)
    od = [_dot(a, jnp.concatenate([v_ref[0, pl.ds(s, win), :], ones_loc], axis=1)) + _dot(c, vc1)
          for a, c, s in zip(p_loc, p_ctx, start)]
    for j, x in zip(rows, od):
        oj = x[:, :LANES] / x[:, LANES:]
        o_ref[0, j * GRID_W:(j + 1) * GRID_W, :] = jnp.where(lo, oj[:GRID_W], oj[GRID_W:]).astype(o_ref.dtype)


def _neighborhood_attention(qg, kv, bias, ctx_len):
    b, t_tot, _ = kv.shape
    t = t_tot - ctx_len
    n_rows = t // GRID_W
    assert n_rows % NA_ROWS == 0
    blk_rows = NA_ROWS * GRID_W
    return pl.pallas_call(
        functools.partial(_na_kernel, ctx_len, n_rows),
        grid=(b, N_PAIRS, n_rows // NA_ROWS),
        in_specs=[pl.BlockSpec((1, blk_rows, LANES), lambda bi, pr, i: (bi, i, pr)),
                  pl.BlockSpec((1, t_tot, LANES), lambda bi, pr, i: (bi, 0, pr)),
                  pl.BlockSpec((1, t_tot, LANES), lambda bi, pr, i: (bi, 0, N_PAIRS + pr)),
                  pl.BlockSpec((NA_KH, 1, 2 * GRID_W, NA_KH * GRID_W), lambda bi, pr, i: (0, pr, 0, 0))],
        out_specs=pl.BlockSpec((1, blk_rows, LANES), lambda bi, pr, i: (bi, i, pr)),
        out_shape=jax.ShapeDtypeStruct((b, t, WIDTH), BF16),
        compiler_params=_params(("parallel", "parallel", "arbitrary")),
        name="neighborhood_attention",
    )(qg, kv, kv, bias)


def _na_bias_table(na_rpb):
    col = np.arange(GRID_W)
    c0 = np.clip(col - NA_KW // 2, 0, GRID_W - NA_KW)
    valid = (col[None, :] >= c0[:, None]) & (col[None, :] < c0[:, None] + NA_KW)
    col_off = np.clip(col[None, :] - col[:, None] + NA_KW - 1, 0, 2 * NA_KW - 2)
    row_off = np.arange(NA_KH)[None, :] - np.arange(NA_KH)[:, None] + NA_KH - 1
    n_col = 2 * NA_KW - 1
    sel_r = (row_off[..., None] == np.arange(2 * NA_KH - 1)).astype(np.float32)
    sel_c = (np.where(valid, col_off, n_col)[..., None] == np.arange(n_col + 1)).astype(np.float32)
    src = jnp.concatenate([na_rpb.astype(F32) * LOG2E, jnp.full(na_rpb.shape[:2] + (1,), NEG_BIG, F32)], axis=-1)
    tbl = jnp.einsum('dri,hij,qcj->dhqrc', sel_r, src, sel_c, precision=lax.Precision.HIGHEST)
    return tbl.reshape(NA_KH, N_PAIRS, 2 * GRID_W, NA_KH * GRID_W)


def _out_kernel(yf_ref, yr_ref, bonus_ref, na_ref, nag_ref, rwg_ref, x_ref, gate_ref, wo_ref,
                gnw_ref, gnb_ref, fg_ref, seg_ref, o_ref):
    seg = seg_ref[...]
    y = yf_ref[0].astype(F32) + yr_ref[0].astype(F32)
    inv_n = 1.0 / HEAD_DIM
    mu = _split_dot_right(y, seg, 2, pack=True) * inv_n
    dev = y - mu
    var = _split_dot_right(dev * dev, seg, 2, pack=True) * inv_n
    yn = dev * lax.rsqrt(var + GN_EPS) * gnw_ref[...] + gnb_ref[...]
    rwg = rwg_ref[0].astype(F32)
    nag = nag_ref[0].astype(F32)
    rw = (yn + bonus_ref[0].astype(F32)) * (rwg * _sigmoid(rwg))
    na = na_ref[0].astype(F32) * (nag * _sigmoid(nag))
    mix = jnp.concatenate([na, rw], axis=-1).astype(BF16)
    h = x_ref[0] + gate_ref[0] * _dot(mix, wo_ref[...])
    ms = jnp.mean(h * h, axis=-1, keepdims=True)
    o_ref[0] = h * lax.rsqrt(ms + RMS_EPS) * fg_ref[...]


OUT_TILE = 1024


def _readout_project(yf, yr, bonus, na, qg, x, gate, w_out_bf16, gn_w, gn_b, final_g, seg):
    b, t, d = x.shape
    lat = lambda i, j: (i, j, 0)
    fixed2 = lambda i, j: (0, 0)
    tok = pl.BlockSpec((1, OUT_TILE, WIDTH), lat)
    return pl.pallas_call(
        _out_kernel,
        grid=(b, t // OUT_TILE),
        in_specs=[tok, tok, tok, tok,
                  pl.BlockSpec((1, OUT_TILE, WIDTH), lambda i, j: (i, j, 1)),
                  pl.BlockSpec((1, OUT_TILE, WIDTH), lambda i, j: (i, j, 2)),
                  pl.BlockSpec((1, OUT_TILE, d), lat),
                  pl.BlockSpec((1, 1, d), lambda i, j: (i, 0, 0)),
                  pl.BlockSpec((2 * WIDTH, d), fixed2),
                  pl.BlockSpec((1, WIDTH), fixed2),
                  pl.BlockSpec((1, WIDTH), fixed2),
                  pl.BlockSpec((1, d), fixed2),
                  pl.BlockSpec((LANES, LANES), fixed2)],
        out_specs=pl.BlockSpec((1, OUT_TILE, d), lat),
        out_shape=jax.ShapeDtypeStruct((b, t, d), F32),
        compiler_params=_params(("parallel", "arbitrary")),
        name="readout_project",
    )(yf, yr, bonus, na, qg, qg, x, gate, w_out_bf16, gn_w, gn_b, final_g, seg)


def _chunk_matrices():
    idx = np.arange(TILE)
    same = (idx[:, None] // CHUNK) == (idx[None, :] // CHUNK)
    tri_f = same & (idx[:, None] >= idx[None, :])
    tri_r = same & (idx[:, None] <= idx[None, :])
    tri = jnp.asarray(np.stack([tri_f, tri_r]).astype(np.float32), dtype=BF16)
    hid = np.arange(LANES) // HEAD_DIM
    seg = jnp.asarray((hid[:, None] == hid[None, :]).astype(np.float32), dtype=BF16)
    return tri, seg


def _lora_cat(w2):
    z = jnp.zeros_like(w2[0])
    return jnp.stack([jnp.concatenate([w2[0], z], axis=0), jnp.concatenate([z, w2[1]], axis=0)]).astype(BF16)


def kernel(x, c, ctx, c_ctx, w_mod, b_mod, norm_g, w_in, conv_w, decay_w0, decay_w2, aaa_a0, aaa_a2,
           k_k, k_a, r_k, gn_w, gn_b, na_rpb, w_out, final_g):
    depth = w_mod.shape[0]
    assert depth == 1, "single-layer block: the context stream is never updated"
    b, t, d = x.shape
    l = ctx.shape[1]
    assert l == TILE and t % OUT_TILE == 0 and t // GRID_W >= NA_KH and w_in.shape[2] == D_IN
    n_ctx_tiles = l // TILE

    rows = -(-(b + 1) // 8) * 8
    cvecs = jnp.zeros((rows, d), F32).at[:b].set(c).at[b].set(c_ctx)
    mod = _modulation(cvecs, w_mod[0], b_mod[0])
    shift, scale, gate = mod[:, :d], mod[:, d:2 * d], mod[:, 2 * d:]

    tri, seg = _chunk_matrices()
    kv, qg, *prep = _project_and_prepare(
        x, ctx, shift[:b].reshape(b, 1, d), scale[:b].reshape(b, 1, d), shift[b:b + 1], scale[b:b + 1],
        norm_g[0].reshape(1, d), w_in[0].astype(BF16), conv_w[0], decay_w0[0], _lora_cat(decay_w2[0]),
        aaa_a0[0], _lora_cat(aaa_a2[0]), k_k[0].reshape(1, WIDTH), k_a[0].reshape(1, WIDTH),
        r_k[0].reshape(1, WIDTH), seg, tri)
    yf, yr = _rwkv_scan(prep, l // CHUNK)
    na = _neighborhood_attention(qg, kv, _na_bias_table(na_rpb[0]), l)
    return _readout_project(yf, yr, prep[-1], na, qg, x, gate[:b].reshape(b, 1, d), w_out[0].astype(BF16),
                            gn_w[0].reshape(1, WIDTH), gn_b[0].reshape(1, WIDTH), final_g.reshape(1, d), seg)
```
